```python
import math
import jax, jax.numpy as jnp
from jax import lax
import numpy as np

D_MODEL = 2048
BATCH = 2
SEQ = 4096
DEPTH = 1

D_INNER_A = 3 * D_MODEL // 2
HEAD_DIM_A = 64
N_HEADS_A = D_INNER_A // HEAD_DIM_A
N_GROUPS_A = 8
HEADS_PER_GROUP = N_HEADS_A // N_GROUPS_A
D_STATE_A = 128
CONV_A = 4
CHUNK = 256
CONV_DIM_A = D_INNER_A + 2 * N_GROUPS_A * D_STATE_A
D_S5 = D_MODEL // 2
S5_GROUP = 16
N_GROUPS_S5 = D_S5 // S5_GROUP
S5_STATE = 64
D_FF = 256 * ((8 * D_MODEL // 3 + 255) // 256)
CONV_FFN = 3
EPS = 1e-6
DT_MIN = 1e-3
DT_MAX = 1e-1
EIG_MAX = -1e-4
IN_COLS = D_INNER_A + CONV_DIM_A + N_HEADS_A + D_S5 + 2 * D_MODEL

kernel_name = "hybrid_ssd_s5_gated_convffn"


def rmsnorm(x, w):
    xf = x.astype(jnp.float32)
    xf = xf * lax.rsqrt(jnp.mean(xf * xf, axis=-1, keepdims=True) + EPS)
    return xf.astype(x.dtype) * w


def causal_dwconv(x, w, b):
    k = w.shape[0]
    y = lax.conv_general_dilated(
        x, w[:, None, :].astype(x.dtype), window_strides=(1,), padding=[(k - 1, 0)],
        dimension_numbers=("NWC", "WIO", "NWC"), feature_group_count=x.shape[-1])
    return y + b


def segsum(a):
    t = a.shape[-1]
    ar = jnp.broadcast_to(a[..., :, None], a.shape + (t,))
    strict = jnp.tril(jnp.ones((t, t), dtype=bool), k=-1)
    cs = jnp.cumsum(jnp.where(strict, ar, 0.0), axis=-2)
    incl = jnp.tril(jnp.ones((t, t), dtype=bool), k=0)
    return jnp.where(incl, cs, -jnp.inf)


def ssd_chunked(xh, da, bm, cm):
    b, seqlen = xh.shape[:2]
    nc = -(-seqlen // CHUNK)
    pad = nc * CHUNK - seqlen
    if pad:
        xh = jnp.pad(xh, ((0, 0), (0, pad), (0, 0), (0, 0), (0, 0)))
        da = jnp.pad(da, ((0, 0), (0, pad), (0, 0), (0, 0)))
        bm = jnp.pad(bm, ((0, 0), (0, pad), (0, 0), (0, 0)))
        cm = jnp.pad(cm, ((0, 0), (0, pad), (0, 0), (0, 0)))
    X = xh.reshape(b, nc, CHUNK, N_GROUPS_A, HEADS_PER_GROUP, HEAD_DIM_A)
    A = da.reshape(b, nc, CHUNK, N_GROUPS_A, HEADS_PER_GROUP).transpose(0, 3, 4, 1, 2)
    Bc = bm.reshape(b, nc, CHUNK, N_GROUPS_A, D_STATE_A)
    Cc = cm.reshape(b, nc, CHUNK, N_GROUPS_A, D_STATE_A)
    a_cs = jnp.cumsum(A, axis=-1)
    decay_in = jnp.exp(segsum(A))
    cb = jnp.einsum("bclgn,bcsgn->bgcls", Cc, Bc)
    y_diag = jnp.einsum("bgcls,bgrcls,bcsgrp->bclgrp", cb, decay_in, X)
    decay_states = jnp.exp(a_cs[..., -1:] - a_cs)
    states = jnp.einsum("bclgn,bgrcl,bclgrp->bcgrpn", Bc, decay_states, X)
    states = jnp.concatenate([jnp.zeros_like(states[:, :1]), states], axis=1)
    chunk_decay = jnp.exp(segsum(jnp.pad(a_cs[..., -1], ((0, 0), (0, 0), (0, 0), (1, 0)))))
    new_states = jnp.einsum("bgrzc,bcgrpn->bzgrpn", chunk_decay, states)
    states = new_states[:, :-1]
    y_off = jnp.einsum("bclgn,bcgrpn,bgrcl->bclgrp", Cc, states, jnp.exp(a_cs))
    y = (y_diag + y_off).reshape(b, nc * CHUNK, N_GROUPS_A, HEADS_PER_GROUP, HEAD_DIM_A)
    return y[:, :seqlen]


def mamba2_mixer(z, xbc, dt_raw, conv_w, conv_b, dt_bias, a_log, d_skip, norm_w, w_proj):
    b, seqlen, _ = z.shape
    f32 = jnp.float32
    xbc = jax.nn.silu(causal_dwconv(xbc, conv_w, conv_b))
    gn = N_GROUPS_A * D_STATE_A
    xs, bm, cm = jnp.split(xbc, [D_INNER_A, D_INNER_A + gn], axis=-1)
    dt = jax.nn.softplus(dt_raw.astype(f32) + dt_bias.astype(f32))
    a = -jnp.exp(a_log.astype(f32))
    dt_g = dt.reshape(b, seqlen, N_GROUPS_A, HEADS_PER_GROUP)
    xh = xs.astype(f32).reshape(b, seqlen, N_GROUPS_A, HEADS_PER_GROUP, HEAD_DIM_A)
    y = ssd_chunked(xh * dt_g[..., None],
                    dt_g * a.reshape(N_GROUPS_A, HEADS_PER_GROUP),
                    bm.astype(f32).reshape(b, seqlen, N_GROUPS_A, D_STATE_A),
                    cm.astype(f32).reshape(b, seqlen, N_GROUPS_A, D_STATE_A))
    y = y + d_skip.astype(f32).reshape(N_GROUPS_A, HEADS_PER_GROUP)[:, :, None] * xh
    y = y.reshape(b, seqlen, D_INNER_A) * jax.nn.silu(z.astype(f32))
    yg = y.reshape(b, seqlen, N_GROUPS_A, D_INNER_A // N_GROUPS_A)
    yg = yg * lax.rsqrt(jnp.mean(yg * yg, axis=-1, keepdims=True) + EPS)
    y = yg.reshape(b, seqlen, D_INNER_A).astype(z.dtype) * norm_w
    return y @ w_proj


def _lin_rec(e1, e2):
    a1, b1 = e1
    a2, b2 = e2
    return a1 * a2, a2 * b1 + b2


def s5_mixer(u, lam_re, lam_im, log_dt, b_re, b_im, c_re, c_im, d_skip, w_glu):
    b, seqlen, _ = u.shape
    f32 = jnp.float32
    ug = u.astype(f32).reshape(b, seqlen, N_GROUPS_S5, S5_GROUP)
    lam = lax.complex(jnp.minimum(lam_re.astype(f32), EIG_MAX), lam_im.astype(f32))
    dt = jnp.exp(log_dt.astype(f32))[:, None]
    lam_bar = jnp.exp(lam * dt)
    bmat = lax.complex(b_re.astype(f32), b_im.astype(f32))
    cmat = lax.complex(c_re.astype(f32), c_im.astype(f32))
    b_bar = ((lam_bar - 1.0) / lam)[..., None] * bmat
    bu = jnp.einsum("gpc,blgc->blgp", b_bar, ug.astype(jnp.complex64))
    a_el = jnp.broadcast_to(lam_bar, bu.shape)
    _, states = lax.associative_scan(_lin_rec, (a_el, bu), axis=1)
    y = jnp.einsum("gcp,blgp->blgc", cmat, states).real + d_skip.astype(f32).reshape(N_GROUPS_S5, S5_GROUP) * ug
    y = jax.nn.gelu(y.reshape(b, seqlen, D_S5)).astype(u.dtype)
    val, gate = jnp.split(y @ w_glu, 2, axis=-1)
    return val * jax.nn.sigmoid(gate)


def conv_glu_ffn(hn, w_up, conv_w, conv_b, w_down):
    up = causal_dwconv(hn @ w_up, conv_w, conv_b)
    gate, val = jnp.split(up, 2, axis=-1)
    return (jax.nn.silu(gate) * val) @ w_down


def setup_inputs(seed: int = 0) -> dict:
    key = jax.random.key(seed)
    ks = jax.random.split(key, 28)
    f32 = jnp.float32
    L = DEPTH

    def nrm(k, shape, scale):
        return jax.random.normal(k, shape, f32) * scale

    log_lo, log_hi = math.log(DT_MIN), math.log(DT_MAX)
    dt0 = jnp.exp(jax.random.uniform(ks[6], (L, N_HEADS_A), f32, log_lo, log_hi))
    dt_bias = dt0 + jnp.log(-jnp.expm1(-dt0))
    lam_im0 = math.pi * jnp.arange(S5_STATE, dtype=f32)
    return {
        "x": nrm(ks[0], (BATCH, SEQ, D_MODEL), 1.0),
        "norm_mix_w": 1.0 + nrm(ks[1], (L, D_MODEL), 0.02),
        "w_in": nrm(ks[2], (L, D_MODEL, IN_COLS), D_MODEL ** -0.5),
        "conv_a_w": nrm(ks[3], (L, CONV_A, CONV_DIM_A), CONV_A ** -0.5),
        "conv_a_b": nrm(ks[4], (L, CONV_DIM_A), 0.01),
        "dt_bias": dt_bias,
        "a_log": jnp.log(jax.random.uniform(ks[7], (L, N_HEADS_A), f32, 1.0, 16.0)),
        "d_a": 1.0 + nrm(ks[8], (L, N_HEADS_A), 0.02),
        "norm_a_w": 1.0 + nrm(ks[9], (L, D_INNER_A), 0.02),
        "w_proj_a": nrm(ks[10], (L, D_INNER_A, D_MODEL), D_INNER_A ** -0.5),
        "s5_lam_re": -0.5 + nrm(ks[11], (L, N_GROUPS_S5, S5_STATE), 0.01),
        "s5_lam_im": lam_im0 + nrm(ks[12], (L, N_GROUPS_S5, S5_STATE), 0.01),
        "s5_log_dt": jax.random.uniform(ks[13], (L, N_GROUPS_S5), f32, log_lo, log_hi),
        "s5_b_re": nrm(ks[14], (L, N_GROUPS_S5, S5_STATE, S5_GROUP), (2 * S5_GROUP) ** -0.5),
        "s5_b_im": nrm(ks[15], (L, N_GROUPS_S5, S5_STATE, S5_GROUP), (2 * S5_GROUP) ** -0.5),
        "s5_c_re": nrm(ks[16], (L, N_GROUPS_S5, S5_GROUP, S5_STATE), S5_STATE ** -0.5),
        "s5_c_im": nrm(ks[17], (L, N_GROUPS_S5, S5_GROUP, S5_STATE), S5_STATE ** -0.5),
        "s5_d": nrm(ks[18], (L, D_S5), 1.0),
        "w_s5_glu": nrm(ks[19], (L, D_S5, 2 * D_MODEL), D_S5 ** -0.5),
        "w_out": nrm(ks[20], (L, D_MODEL, D_MODEL), D_MODEL ** -0.5),
        "norm_ffn_w": 1.0 + nrm(ks[21], (L, D_MODEL), 0.02),
        "w_up": nrm(ks[22], (L, D_MODEL, 2 * D_FF), D_MODEL ** -0.5),
        "conv_ffn_w": nrm(ks[23], (L, CONV_FFN, 2 * D_FF), CONV_FFN ** -0.5),
        "conv_ffn_b": nrm(ks[24], (L, 2 * D_FF), 0.01),
        "w_down": nrm(ks[25], (L, D_FF, D_MODEL), D_FF ** -0.5),
        "norm_final_w": 1.0 + nrm(ks[26], (D_MODEL,), 0.02),
    }


def reference(x, norm_mix_w, w_in, conv_a_w, conv_a_b, dt_bias, a_log, d_a, norm_a_w, w_proj_a,
              s5_lam_re, s5_lam_im, s5_log_dt, s5_b_re, s5_b_im, s5_c_re, s5_c_im, s5_d, w_s5_glu,
              w_out, norm_ffn_w, w_up, conv_ffn_w, conv_ffn_b, w_down, norm_final_w):
    sizes = [D_INNER_A, CONV_DIM_A, N_HEADS_A, D_S5, D_MODEL, D_MODEL]
    splits = [int(s) for s in np.cumsum(sizes)[:-1]]
    h = x
    for i in range(DEPTH):
        hn = rmsnorm(h, norm_mix_w[i])
        z, xbc, dt_raw, u, g_a, g_b = jnp.split(hn @ w_in[i], splits, axis=-1)
        y_a = mamba2_mixer(z, xbc, dt_raw, conv_a_w[i], conv_a_b[i], dt_bias[i], a_log[i], d_a[i],
                           norm_a_w[i], w_proj_a[i])
        y_b = s5_mixer(u, s5_lam_re[i], s5_lam_im[i], s5_log_dt[i], s5_b_re[i], s5_b_im[i],
                       s5_c_re[i], s5_c_im[i], s5_d[i], w_s5_glu[i])
        merged = jax.nn.sigmoid(g_a) * y_a + jax.nn.sigmoid(g_b) * y_b
        h = h + merged @ w_out[i]
        hn = rmsnorm(h, norm_ffn_w[i])
        h = h + conv_glu_ffn(hn, w_up[i], conv_ffn_w[i], conv_ffn_b[i], w_down[i])
    return rmsnorm(h, norm_final_w)
```

```python
import functools
import math

import jax
import jax.numpy as jnp
from jax import lax
from jax.experimental import pallas as pl
from jax.experimental.pallas import tpu as pltpu

F32 = jnp.float32
BF16 = jnp.bfloat16

D_MODEL = 2048
D_INNER = 3072
HEAD_DIM = 64
N_HEADS = 48
N_GROUPS = 8
HEADS_PER_GROUP = 6
GROUP_CH = D_INNER // N_GROUPS
D_STATE = 128
CONV_A = 4
CHUNK = 256
D_S5 = 1024
S5_GROUP = 16
S5_NG = 64
S5_STATE = 64
D_FF = 5632
CONV_FFN = 3
EPS = 1e-6
EIG_MAX = -1e-4

LANES = 128
SUBLANES = 8
BF16_ROWS = 16
VMEM_LIMIT = 56 * 1024 * 1024

COL_Z = 0
COL_XS = 3072
COL_GA = 6144
COL_GB = 8192
COL_B = 10240
COL_C = 11264
COL_U = 12288
N_MAIN = 13312
DT_LANE_STRIDE = 16


def _sigmoid(v):
    return 1.0 / (1.0 + jnp.exp(-v))


def _silu(v):
    return v * _sigmoid(v)


def _softplus(v):
    return jnp.maximum(v, 0.0) + jnp.log1p(jnp.exp(-jnp.abs(v)))


def _gelu_tanh(v):
    c = math.sqrt(2.0 / math.pi)
    return 0.5 * v * (1.0 + jnp.tanh(c * (v + 0.044715 * (v * v * v))))


def _split3(v):
    hi = v.astype(BF16)
    r1 = v - hi.astype(F32)
    mid = r1.astype(BF16)
    lo = (r1 - mid.astype(F32)).astype(BF16)
    return hi, mid, lo


def _dot(a, b):
    return jnp.dot(a, b, preferred_element_type=F32)


def _dot3(v, m_bf16):
    hi, mid, lo = _split3(v)
    return _dot(hi, m_bf16) + _dot(mid, m_bf16) + _dot(lo, m_bf16)


def _cparams(n_axes):
    return pltpu.CompilerParams(dimension_semantics=("arbitrary",) * n_axes,
                                vmem_limit_bytes=VMEM_LIMIT)


def _inproj_body(x_ref, nw_ref, w_ref, wdt_ref, o_ref, dt_ref, hn_scr):
    @pl.when(pl.program_id(1) == 0)
    def _():
        x = x_ref[...]
        ms = jnp.mean(x * x, axis=-1, keepdims=True)
        hn = ((x * lax.rsqrt(ms + EPS)) * nw_ref[...]).astype(BF16)
        hn_scr[...] = hn
        dt_ref[...] = _dot(hn, wdt_ref[...])

    o_ref[...] = _dot(hn_scr[...], w_ref[...]).astype(o_ref.dtype)


def _inproj(x2, norm_w, w_main, w_dt, tm=1024, tn=1024):
    t = x2.shape[0]
    return pl.pallas_call(
        _inproj_body,
        grid=(t // tm, N_MAIN // tn),
        in_specs=[
            pl.BlockSpec((tm, D_MODEL), lambda i, j: (i, 0)),
            pl.BlockSpec((1, D_MODEL), lambda i, j: (0, 0)),
            pl.BlockSpec((D_MODEL, tn), lambda i, j: (0, j)),
            pl.BlockSpec((D_MODEL, LANES), lambda i, j: (0, 0)),
        ],
        out_specs=[
            pl.BlockSpec((tm, tn), lambda i, j: (i, j)),
            pl.BlockSpec((tm, LANES), lambda i, j: (i, 0)),
        ],
        out_shape=[
            jax.ShapeDtypeStruct((t, N_MAIN), BF16),
            jax.ShapeDtypeStruct((t, LANES), F32),
        ],
        scratch_shapes=[pltpu.VMEM((tm, D_MODEL), BF16)],
        compiler_params=_cparams(2),
        name="inproj",
    )(x2, norm_w, w_main, w_dt)


def _mamba_body(z_ref, xs_ref, b_ref, c_ref, ga_ref, dtraw_ref,
                cwx_ref, cbx_ref, cwb_ref, cbb_ref, cwc_ref, cbc_ref,
                dtb_ref, aneg_ref, dexp_ref, e_ref, nw_ref, wproj_ref,
                o_ref,
                xpx, xpb, xpc, zg, st_scr, yn_scr, csg, cstg, dtbf, ecsbf, dstbf, ecsl):
    q = CHUNK
    halo = SUBLANES

    @pl.when(pl.program_id(1) == 0)
    def _():
        xpx[:, 0:halo, :] = jnp.zeros((N_GROUPS, halo, GROUP_CH), F32)
        xpb[:, 0:halo, :] = jnp.zeros((N_GROUPS, halo, D_STATE), F32)
        xpc[:, 0:halo, :] = jnp.zeros((N_GROUPS, halo, D_STATE), F32)
        st_scr[...] = jnp.zeros_like(st_scr)

    for g in range(N_GROUPS):
        xpx[g, halo:halo + q, :] = xs_ref[:, g * GROUP_CH:(g + 1) * GROUP_CH].astype(F32)
        xpb[g, halo:halo + q, :] = b_ref[:, g * D_STATE:(g + 1) * D_STATE].astype(F32)
        xpc[g, halo:halo + q, :] = c_ref[:, g * D_STATE:(g + 1) * D_STATE].astype(F32)
        zg[g] = z_ref[:, g * GROUP_CH:(g + 1) * GROUP_CH]

    dt = _softplus(dtraw_ref[...] + dtb_ref[...])
    a = aneg_ref[...] * dt
    row = lax.broadcasted_iota(jnp.int32, (q, q), 0)
    col = lax.broadcasted_iota(jnp.int32, (q, q), 1)
    causal = row >= col
    tril = jnp.where(causal, 1.0, 0.0).astype(BF16)
    ah, am, al = _split3(a)
    cs = _dot(tril, ah) + _dot(tril, am) + _dot(tril, al)
    ecs = jnp.exp(cs)
    dtbf[...] = dt.astype(BF16)
    ecsbf[...] = ecs.astype(BF16)
    dstbf[...] = jnp.exp(cs[q - 1:q, :] - cs).astype(BF16)
    ecsl[...] = jnp.broadcast_to(ecs[q - 1:q, :], (SUBLANES, LANES))
    for g in range(N_GROUPS):
        cs_g = cs if g == 0 else pltpu.roll(cs, LANES - DT_LANE_STRIDE * g, axis=1)
        csg[g] = cs_g
        cstg[g] = cs_g.T

    def conv_silu(xp, g, cw_ref, cb_ref):
        acc = cb_ref[g] + cw_ref[g, 3:4, :] * xp[g, pl.ds(halo, q), :]
        acc = acc + cw_ref[g, 2:3, :] * xp[g, pl.ds(halo - 1, q), :]
        acc = acc + cw_ref[g, 1:2, :] * xp[g, pl.ds(halo - 2, q), :]
        acc = acc + cw_ref[g, 0:1, :] * xp[g, pl.ds(halo - 3, q), :]
        xp[g, 0:halo, :] = xp[g, q:q + halo, :]
        return _silu(acc)

    lane = lax.broadcasted_iota(jnp.int32, (q, LANES), 1)
    first_half = lane < HEAD_DIM

    def group_body(g, carry):
        xs_g = conv_silu(xpx, g, cwx_ref, cbx_ref)
        b_g = conv_silu(xpb, g, cwb_ref, cbb_ref)
        c_g = conv_silu(xpc, g, cwc_ref, cbc_ref)
        e_g = e_ref[g]
        dtx = _dot(dtbf[...], e_g)
        ecsx = _dot(ecsbf[...], e_g)
        dstx = _dot(dstbf[...], e_g)
        sdec = _dot3(ecsl[...], e_g)[0:1, :]
        xdt = xs_g * dtx
        x_bf = xdt.astype(BF16)
        xd_bf = (xdt * dstx).astype(BF16)
        b_bf = b_g.astype(BF16)
        c_bf = c_g.astype(BF16)
        cb = lax.dot_general(c_bf, b_bf, (((1,), (1,)), ((), ())), preferred_element_type=F32)
        cs_cols = csg[g]
        cs_rows = cstg[g]
        pieces = []
        for p in range(HEADS_PER_GROUP // 2):
            xp_ = x_bf[:, p * LANES:(p + 1) * LANES]
            acc = None
            for k in range(2):
                r = 2 * p + k
                diff = cs_cols[:, r:r + 1] - cs_rows[r:r + 1, :]
                m = (jnp.exp(jnp.where(causal, diff, -jnp.inf)) * cb).astype(BF16)
                keep = first_half if k == 0 else jnp.logical_not(first_half)
                xk = jnp.where(keep, xp_, jnp.zeros_like(xp_))
                t = _dot(m, xk)
                acc = t if acc is None else acc + t
            pieces.append(acc)
        y = jnp.concatenate(pieces, axis=1)
        s_old = st_scr[g]
        y = y + _dot(c_bf, s_old.astype(BF16)) * ecsx
        st_scr[g] = sdec * s_old + _dot(b_g.T.astype(BF16), xd_bf)
        y = y + dexp_ref[g] * xs_g
        y = y * _silu(zg[g].astype(F32))
        ms = jnp.mean(y * y, axis=-1, keepdims=True)
        yn_scr[g] = ((y * lax.rsqrt(ms + EPS)) * nw_ref[g]).astype(BF16)
        return carry

    lax.fori_loop(0, N_GROUPS, group_body, 0)

    yn = jnp.concatenate([yn_scr[g] for g in range(N_GROUPS)], axis=1)
    ya = _dot(yn, wproj_ref[...])
    o_ref[...] = (ya * _sigmoid(ga_ref[...].astype(F32))).astype(o_ref.dtype)


def _mamba(proj, dt_raw, p, batch, seqlen):
    q = CHUNK
    nc = seqlen // q
    t = batch * seqlen

    def rows(b, c):
        return b * nc + c

    def colblk(width, off):
        return pl.BlockSpec((q, width), lambda b, c: (rows(b, c), off // width))

    def full(shape):
        return pl.BlockSpec(shape, lambda b, c: (0,) * len(shape))

    return pl.pallas_call(
        _mamba_body,
        grid=(batch, nc),
        in_specs=[
            colblk(D_INNER, COL_Z), colblk(D_INNER, COL_XS),
            colblk(N_GROUPS * D_STATE, COL_B), colblk(N_GROUPS * D_STATE, COL_C),
            colblk(D_MODEL, COL_GA),
            pl.BlockSpec((q, LANES), lambda b, c: (rows(b, c), 0)),
            full((N_GROUPS, CONV_A, GROUP_CH)), full((N_GROUPS, 1, GROUP_CH)),
            full((N_GROUPS, CONV_A, D_STATE)), full((N_GROUPS, 1, D_STATE)),
            full((N_GROUPS, CONV_A, D_STATE)), full((N_GROUPS, 1, D_STATE)),
            full((1, LANES)), full((1, LANES)),
            full((N_GROUPS, 1, GROUP_CH)), full((N_GROUPS, LANES, GROUP_CH)),
            full((N_GROUPS, 1, GROUP_CH)), full((D_INNER, D_MODEL)),
        ],
        out_specs=pl.BlockSpec((q, D_MODEL), lambda b, c: (rows(b, c), 0)),
        out_shape=jax.ShapeDtypeStruct((t, D_MODEL), BF16),
        scratch_shapes=[
            pltpu.VMEM((N_GROUPS, q + SUBLANES, GROUP_CH), F32),
            pltpu.VMEM((N_GROUPS, q + SUBLANES, D_STATE), F32),
            pltpu.VMEM((N_GROUPS, q + SUBLANES, D_STATE), F32),
            pltpu.VMEM((N_GROUPS, q, GROUP_CH), BF16),
            pltpu.VMEM((N_GROUPS, D_STATE, GROUP_CH), F32),
            pltpu.VMEM((N_GROUPS, q, GROUP_CH), BF16),
            pltpu.VMEM((N_GROUPS, q, LANES), F32),
            pltpu.VMEM((N_GROUPS, LANES, q), F32),
            pltpu.VMEM((q, LANES), BF16),
            pltpu.VMEM((q, LANES), BF16),
            pltpu.VMEM((q, LANES), BF16),
            pltpu.VMEM((SUBLANES, LANES), F32),
        ],
        compiler_params=_cparams(2),
        name="mamba",
    )(proj, proj, proj, proj, proj, dt_raw,
      p["cwx"], p["cbx"], p["cwb"], p["cbb"], p["cwc"], p["cbc"],
      p["dtb"], p["aneg"], p["dexp"], p["e"], p["nw"], p["wproj"])


S5_BLOCKS = D_S5 // LANES
S5_SLABS = 2 * (S5_NG // S5_BLOCKS) * S5_STATE // LANES
S5_HALF = S5_SLABS // 2


def _s5_body(u_ref, gb_ref, ya_ref, wb_ref, wc_ref, lr_ref, li_ref, d_ref, wglu_ref,
             o_ref, xs_scr, st_scr, y_scr, *, tc):
    @pl.when(pl.program_id(1) == 0)
    def _():
        st_scr[...] = jnp.zeros_like(st_scr)

    u = u_ref[...]
    for s in range(S5_BLOCKS):
        bu = _dot(u[:, s * LANES:(s + 1) * LANES], wb_ref[s])
        for k in range(S5_SLABS):
            xs_scr[k, pl.ds(s, tc, stride=SUBLANES), :] = bu[:, k * LANES:(k + 1) * LANES]

    lr = [lr_ref[k] for k in range(S5_HALF)]
    li = [li_ref[k] for k in range(S5_HALF)]

    def step(t, st):
        r0 = pl.multiple_of(t * SUBLANES, SUBLANES)
        new_re, new_im = [], []
        for k in range(S5_HALF):
            re, im = st[k], st[S5_HALF + k]
            nre = (lr[k] * re - li[k] * im) + xs_scr[k, pl.ds(r0, SUBLANES), :]
            nim = (lr[k] * im + li[k] * re) + xs_scr[S5_HALF + k, pl.ds(r0, SUBLANES), :]
            xs_scr[k, pl.ds(r0, SUBLANES), :] = nre
            xs_scr[S5_HALF + k, pl.ds(r0, SUBLANES), :] = nim
            new_re.append(nre)
            new_im.append(nim)
        return tuple(new_re + new_im)

    st0 = tuple(st_scr[k] for k in range(S5_SLABS))
    st = lax.fori_loop(0, tc, step, st0, unroll=8)
    for k in range(S5_SLABS):
        st_scr[k] = st[k]

    for s in range(S5_BLOCKS):
        xb = jnp.concatenate(
            [xs_scr[k, pl.ds(s, tc, stride=SUBLANES), :] for k in range(S5_SLABS)], axis=1)
        y_scr[:, s * LANES:(s + 1) * LANES] = _dot(xb.astype(BF16), wc_ref[s])

    y = y_scr[...] + d_ref[...] * u.astype(F32)
    y = _gelu_tanh(y).astype(BF16)
    vg = _dot(y, wglu_ref[...])
    yb = vg[:, :D_MODEL] * _sigmoid(vg[:, D_MODEL:])
    merged = ya_ref[...].astype(F32) + _sigmoid(gb_ref[...].astype(F32)) * yb
    o_ref[...] = merged.astype(o_ref.dtype)


def _s5(proj, ya_gated, p, batch, seqlen, tc=256):
    nc = seqlen // tc
    t = batch * seqlen

    def rows(b, c):
        return b * nc + c

    def full(shape):
        return pl.BlockSpec(shape, lambda b, c: (0,) * len(shape))

    return pl.pallas_call(
        functools.partial(_s5_body, tc=tc),
        grid=(batch, nc),
        in_specs=[
            pl.BlockSpec((tc, D_S5), lambda b, c: (rows(b, c), COL_U // D_S5)),
            pl.BlockSpec((tc, D_MODEL), lambda b, c: (rows(b, c), COL_GB // D_MODEL)),
            pl.BlockSpec((tc, D_MODEL), lambda b, c: (rows(b, c), 0)),
            full((S5_BLOCKS, LANES, S5_SLABS * LANES)),
            full((S5_BLOCKS, S5_SLABS * LANES, LANES)),
            full((S5_HALF, SUBLANES, LANES)), full((S5_HALF, SUBLANES, LANES)),
            full((1, D_S5)), full((D_S5, 2 * D_MODEL)),
        ],
        out_specs=pl.BlockSpec((tc, D_MODEL), lambda b, c: (rows(b, c), 0)),
        out_shape=jax.ShapeDtypeStruct((t, D_MODEL), BF16),
        scratch_shapes=[
            pltpu.VMEM((S5_SLABS, tc * SUBLANES, LANES), F32),
            pltpu.VMEM((S5_SLABS, SUBLANES, LANES), F32),
            pltpu.VMEM((tc, D_S5), F32),
        ],
        compiler_params=_cparams(2),
        name="s5",
    )(proj, proj, ya_gated, p["wb"], p["wc"], p["lr"], p["li"], p["d"], p["wglu"])


def _merge_body(x_ref, m_ref, w_ref, o_ref):
    o_ref[...] = x_ref[...] + _dot(m_ref[...], w_ref[...])


def _merge_proj(x2, merged, w_out, tm=512):
    t = x2.shape[0]
    return pl.pallas_call(
        _merge_body,
        grid=(t // tm,),
        in_specs=[
            pl.BlockSpec((tm, D_MODEL), lambda i: (i, 0)),
            pl.BlockSpec((tm, D_MODEL), lambda i: (i, 0)),
            pl.BlockSpec((D_MODEL, D_MODEL), lambda i: (0, 0)),
        ],
        out_specs=pl.BlockSpec((tm, D_MODEL), lambda i: (i, 0)),
        out_shape=jax.ShapeDtypeStruct((t, D_MODEL), F32),
        compiler_params=_cparams(1),
        name="merge_proj",
    )(x2, merged, w_out)


def _ffn_body(h_ref, halo_ref, nw_ref, wg_ref, wv_ref, cwg_ref, cwv_ref, cbg_ref, cbv_ref,
              wd_ref, fw_ref, o_ref, lhs_scr, upg_scr, upv_scr, acc_scr, *, tm, tiles_per_seq):
    i = pl.program_id(0)
    j = pl.program_id(1)
    halo = BF16_ROWS

    def norm(v):
        ms = jnp.mean(v * v, axis=-1, keepdims=True)
        return ((v * lax.rsqrt(ms + EPS)) * nw_ref[...]).astype(BF16)

    @pl.when(j == 0)
    def _():
        lhs_scr[halo:halo + tm, :] = norm(h_ref[...])
        first = (i % tiles_per_seq) == 0
        hn_halo = norm(halo_ref[...])
        lhs_scr[0:halo, :] = jnp.where(first, jnp.zeros_like(hn_halo), hn_halo)
        acc_scr[...] = jnp.zeros_like(acc_scr)

    lhs = lhs_scr[...]
    upg_scr[...] = _dot(lhs, wg_ref[...])
    upv_scr[...] = _dot(lhs, wv_ref[...])

    def conv(up_scr, cw_ref, cb_ref):
        acc = cb_ref[...] + cw_ref[2:3, :] * up_scr[pl.ds(halo, tm), :]
        acc = acc + cw_ref[1:2, :] * up_scr[pl.ds(halo - 1, tm), :]
        acc = acc + cw_ref[0:1, :] * up_scr[pl.ds(halo - 2, tm), :]
        return acc

    act = (_silu(conv(upg_scr, cwg_ref, cbg_ref)) * conv(upv_scr, cwv_ref, cbv_ref)).astype(BF16)
    acc_scr[...] += _dot(act, wd_ref[...])

    @pl.when(j == pl.num_programs(1) - 1)
    def _():
        h2 = h_ref[...] + acc_scr[...]
        ms = jnp.mean(h2 * h2, axis=-1, keepdims=True)
        o_ref[...] = (h2 * lax.rsqrt(ms + EPS)) * fw_ref[...]


def _ffn(h1, norm_w, w_up, conv_w, conv_b, w_down, final_w, seqlen, tm=512, tn=512):
    t = h1.shape[0]
    nj = D_FF // tn
    halo = BF16_ROWS
    halo_blocks = tm // halo
    return pl.pallas_call(
        functools.partial(_ffn_body, tm=tm, tiles_per_seq=seqlen // tm),
        grid=(t // tm, nj),
        in_specs=[
            pl.BlockSpec((tm, D_MODEL), lambda i, j: (i, 0)),
            pl.BlockSpec((halo, D_MODEL), lambda i, j: (jnp.maximum(i * halo_blocks - 1, 0), 0)),
            pl.BlockSpec((1, D_MODEL), lambda i, j: (0, 0)),
            pl.BlockSpec((D_MODEL, tn), lambda i, j: (0, j)),
            pl.BlockSpec((D_MODEL, tn), lambda i, j: (0, nj + j)),
            pl.BlockSpec((CONV_FFN, tn), lambda i, j: (0, j)),
            pl.BlockSpec((CONV_FFN, tn), lambda i, j: (0, nj + j)),
            pl.BlockSpec((1, tn), lambda i, j: (0, j)),
            pl.BlockSpec((1, tn), lambda i, j: (0, nj + j)),
            pl.BlockSpec((tn, D_MODEL), lambda i, j: (j, 0)),
            pl.BlockSpec((1, D_MODEL), lambda i, j: (0, 0)),
        ],
        out_specs=pl.BlockSpec((tm, D_MODEL), lambda i, j: (i, 0)),
        out_shape=jax.ShapeDtypeStruct((t, D_MODEL), F32),
        scratch_shapes=[
            pltpu.VMEM((tm + halo, D_MODEL), BF16),
            pltpu.VMEM((tm + halo, tn), F32),
            pltpu.VMEM((tm + halo, tn), F32),
            pltpu.VMEM((tm, D_MODEL), F32),
        ],
        compiler_params=_cparams(2),
        name="ffn",
    )(h1, h1, norm_w, w_up, w_up, conv_w, conv_w, conv_b, conv_b, w_down, final_w)


def _mamba_params(conv_w, conv_b, dt_bias, a_log, d_a, norm_w, w_proj):
    gn = N_GROUPS * D_STATE

    def grouped(v, width):
        lead = v.shape[:-1]
        v = v.reshape(lead + (N_GROUPS, width))
        return jnp.moveaxis(v, -2, 0)

    def head_lanes(v):
        v = v.reshape(N_GROUPS, HEADS_PER_GROUP)
        v = jnp.pad(v, ((0, 0), (0, DT_LANE_STRIDE - HEADS_PER_GROUP)))
        return v.reshape(1, LANES)

    lane_head = jnp.arange(LANES)
    ch_head = jnp.arange(GROUP_CH) // HEAD_DIM
    e = []
    for g in range(N_GROUPS):
        e.append((lane_head[:, None] == (DT_LANE_STRIDE * g + ch_head)[None, :]))
    e = jnp.stack(e).astype(BF16)
    return {
        "cwx": grouped(conv_w[:, :D_INNER], GROUP_CH),
        "cbx": grouped(conv_b[None, :D_INNER], GROUP_CH),
        "cwb": grouped(conv_w[:, D_INNER:D_INNER + gn], D_STATE),
        "cbb": grouped(conv_b[None, D_INNER:D_INNER + gn], D_STATE),
        "cwc": grouped(conv_w[:, D_INNER + gn:], D_STATE),
        "cbc": grouped(conv_b[None, D_INNER + gn:], D_STATE),
        "dtb": head_lanes(dt_bias),
        "aneg": head_lanes(-jnp.exp(a_log)),
        "dexp": grouped(jnp.repeat(d_a, HEAD_DIM)[None, :], GROUP_CH),
        "e": e,
        "nw": grouped(norm_w[None, :], GROUP_CH),
        "wproj": w_proj.astype(BF16),
    }


def _s5_params(lam_re, lam_im, log_dt, b_re, b_im, c_re, c_im, d_skip, w_glu):
    lre = jnp.minimum(lam_re, EIG_MAX)
    lim = lam_im
    dt = jnp.exp(log_dt)[:, None]
    mag = jnp.exp(lre * dt)
    lbr = mag * jnp.cos(lim * dt)
    lbi = mag * jnp.sin(lim * dt)
    nr, ni = lbr - 1.0, lbi
    den = lre * lre + lim * lim
    cr = (nr * lre + ni * lim) / den
    ci = (ni * lre - nr * lim) / den
    bbr = cr[..., None] * b_re - ci[..., None] * b_im
    bbi = cr[..., None] * b_im + ci[..., None] * b_re
    gl = S5_NG // S5_BLOCKS
    eye = jnp.eye(gl, dtype=F32)

    def in_map(v):
        v = v.reshape(S5_BLOCKS, gl, S5_STATE, S5_GROUP).transpose(0, 1, 3, 2)
        v = jnp.einsum("sgcp,gh->sgchp", v, eye)
        return v.reshape(S5_BLOCKS, gl * S5_GROUP, gl * S5_STATE)

    def out_map(v):
        v = v.reshape(S5_BLOCKS, gl, S5_GROUP, S5_STATE).transpose(0, 1, 3, 2)
        v = jnp.einsum("sgpc,gh->sgphc", v, eye)
        return v.reshape(S5_BLOCKS, gl * S5_STATE, gl * S5_GROUP)

    def scan_coef(v):
        return v.reshape(S5_BLOCKS, S5_HALF, LANES).transpose(1, 0, 2)

    return {
        "wb": jnp.concatenate([in_map(bbr), in_map(bbi)], axis=2).astype(BF16),
        "wc": jnp.concatenate([out_map(c_re), -out_map(c_im)], axis=1).astype(BF16),
        "lr": scan_coef(lbr),
        "li": scan_coef(lbi),
        "d": d_skip[None, :],
        "wglu": w_glu.astype(BF16),
    }


def _inproj_weights(w_in):
    gn = N_GROUPS * D_STATE
    o_xbc = D_INNER
    o_dt = o_xbc + D_INNER + 2 * gn
    o_u = o_dt + N_HEADS
    o_ga = o_u + D_S5
    o_gb = o_ga + D_MODEL
    seg = lambda a, n: w_in[:, a:a + n]
    w_main = jnp.concatenate([
        seg(0, D_INNER), seg(o_xbc, D_INNER), seg(o_ga, D_MODEL), seg(o_gb, D_MODEL),
        seg(o_xbc + D_INNER, gn), seg(o_xbc + D_INNER + gn, gn), seg(o_u, D_S5)], axis=1).astype(BF16)
    w_dt = seg(o_dt, N_HEADS).reshape(D_MODEL, N_GROUPS, HEADS_PER_GROUP)
    w_dt = jnp.pad(w_dt, ((0, 0), (0, 0), (0, DT_LANE_STRIDE - HEADS_PER_GROUP)))
    return w_main, w_dt.reshape(D_MODEL, LANES).astype(BF16)


def kernel(x, norm_mix_w, w_in, conv_a_w, conv_a_b, dt_bias, a_log, d_a, norm_a_w, w_proj_a,
           s5_lam_re, s5_lam_im, s5_log_dt, s5_b_re, s5_b_im, s5_c_re, s5_c_im, s5_d, w_s5_glu,
           w_out, norm_ffn_w, w_up, conv_ffn_w, conv_ffn_b, w_down, norm_final_w):
    batch, seqlen, _ = x.shape
    assert w_in.shape[0] == 1, "single-layer block: the final norm is fused into the FFN stage"
    h = x.reshape(batch * seqlen, D_MODEL)
    w_main, w_dt = _inproj_weights(w_in[0])
    proj, dt_raw = _inproj(h, norm_mix_w[0][None, :], w_main, w_dt)
    mp = _mamba_params(conv_a_w[0], conv_a_b[0], dt_bias[0], a_log[0], d_a[0], norm_a_w[0],
                       w_proj_a[0])
    ya_gated = _mamba(proj, dt_raw, mp, batch, seqlen)
    sp = _s5_params(s5_lam_re[0], s5_lam_im[0], s5_log_dt[0], s5_b_re[0], s5_b_im[0],
                    s5_c_re[0], s5_c_im[0], s5_d[0], w_s5_glu[0])
    merged = _s5(proj, ya_gated, sp, batch, seqlen)
    h1 = _merge_proj(h, merged, w_out[0].astype(BF16))
    out = _ffn(h1, norm_ffn_w[0][None, :], w_up[0].astype(BF16), conv_ffn_w[0],
               conv_ffn_b[0][None, :], w_down[0].astype(BF16), norm_final_w[None, :], seqlen)
    return out.reshape(batch, seqlen, D_MODEL)
```

```python
import functools
import math

import jax
import jax.numpy as jnp
from jax import lax
from jax.experimental import pallas as pl
from jax.experimental.pallas import tpu as pltpu

F32 = jnp.float32
BF16 = jnp.bfloat16

D_MODEL = 2048
D_INNER = 3072
HEAD_DIM = 64
N_HEADS = 48
N_GROUPS = 8
HEADS_PER_GROUP = 6
GROUP_CH = D_INNER // N_GROUPS
D_STATE = 128
CONV_A = 4
CHUNK = 256
D_S5 = 1024
S5_GROUP = 16
S5_NG = 64
S5_STATE = 64
D_FF = 5632
CONV_FFN = 3
EPS = 1e-6
EIG_MAX = -1e-4

LANES = 128
SUBLANES = 8
BF16_ROWS = 16
VMEM_LIMIT = 56 * 1024 * 1024

COL_Z = 0
COL_XS = 3072
COL_B = 6144
COL_C = 7168
COL_GA = 8192
COL_GB = 10240
COL_U = 12288
N_MAIN = 13312
N_XBC = D_INNER + 2 * N_GROUPS * D_STATE
N_DIRECT = D_INNER + N_XBC
DT_LANE_STRIDE = 16
NEG_LOG2E = -1.4426950408889634


def _sigmoid(v):
    return 1.0 / (1.0 + jnp.exp2(v * NEG_LOG2E))


def _silu(v):
    return v * _sigmoid(v)


def _softplus(v):
    return jnp.maximum(v, 0.0) + jnp.log1p(jnp.exp(-jnp.abs(v)))


def _gelu_tanh(v):
    c = math.sqrt(2.0 / math.pi)
    return 0.5 * v * (1.0 + jnp.tanh(c * (v + 0.044715 * (v * v * v))))


def _split3(v):
    hi = v.astype(BF16)
    r1 = v - hi.astype(F32)
    mid = r1.astype(BF16)
    lo = (r1 - mid.astype(F32)).astype(BF16)
    return hi, mid, lo


def _dot(a, b):
    return jnp.dot(a, b, preferred_element_type=F32)


def _dot3(v, m_bf16):
    hi, mid, lo = _split3(v)
    return _dot(hi, m_bf16) + _dot(mid, m_bf16) + _dot(lo, m_bf16)


def _cparams(n_axes):
    return pltpu.CompilerParams(dimension_semantics=("arbitrary",) * n_axes,
                                vmem_limit_bytes=VMEM_LIMIT)


def _norm_body(x_ref, nw_ref, wdt_ref, hn_ref, dt_ref):
    x = x_ref[...]
    ms = jnp.mean(x * x, axis=-1, keepdims=True)
    hn = ((x * lax.rsqrt(ms + EPS)) * nw_ref[...]).astype(BF16)
    hn_ref[...] = hn
    dt_ref[...] = _dot(hn, wdt_ref[...])


def _norm_dt(x2, norm_w, w_dt, tm=512):
    t = x2.shape[0]
    return pl.pallas_call(
        _norm_body,
        grid=(t // tm,),
        in_specs=[
            pl.BlockSpec((tm, D_MODEL), lambda i: (i, 0)),
            pl.BlockSpec((1, D_MODEL), lambda i: (0, 0)),
            pl.BlockSpec((D_MODEL, LANES), lambda i: (0, 0)),
        ],
        out_specs=[
            pl.BlockSpec((tm, D_MODEL), lambda i: (i, 0)),
            pl.BlockSpec((tm, LANES), lambda i: (i, 0)),
        ],
        out_shape=[
            jax.ShapeDtypeStruct((t, D_MODEL), BF16),
            jax.ShapeDtypeStruct((t, LANES), F32),
        ],
        compiler_params=_cparams(1),
        name="norm_dt",
    )(x2, norm_w, w_dt)


IN_TILE = 1024
IN_SUB = 256
J_XS = COL_XS // IN_TILE
J_TAIL = N_DIRECT // IN_TILE
J_U = COL_U // IN_TILE


def _inproj_body(hn_ref, w_ref, wt_ref, cw_ref, cb_ref, o_ref, wbf, xp, carry, *, tm,
                 tiles_per_seq):
    j = pl.program_id(0)
    i = pl.program_id(1)
    halo = SUBLANES

    @pl.when((i == 0) & (j < J_TAIL))
    def _():
        wbf[...] = w_ref[...].astype(BF16)

    def tiles(w_src, epilogue):
        hn = hn_ref[...]
        for n in range(IN_TILE // IN_SUB):
            sl = slice(n * IN_SUB, (n + 1) * IN_SUB)
            r = _dot(hn, w_src[:, sl])
            o_ref[:, sl] = epilogue(r, sl).astype(o_ref.dtype)

    @pl.when(j < J_XS)
    def _():
        tiles(wbf, lambda r, sl: _silu(r))

    @pl.when((j >= J_XS) & (j < J_TAIL))
    def _():
        first = (i % tiles_per_seq) == 0

        def conv_silu(r, sl):
            prev = carry[:, sl]
            xp[0:halo, sl] = jnp.where(first, jnp.zeros_like(prev), prev)
            xp[halo:halo + tm, sl] = r
            carry[:, sl] = r[tm - halo:tm, :]
            acc = cb_ref[:, sl] + cw_ref[3:4, sl] * r
            acc = acc + cw_ref[2:3, sl] * xp[pl.ds(halo - 1, tm), sl]
            acc = acc + cw_ref[1:2, sl] * xp[pl.ds(halo - 2, tm), sl]
            acc = acc + cw_ref[0:1, sl] * xp[pl.ds(halo - 3, tm), sl]
            return _silu(acc)

        tiles(wbf, conv_silu)

    @pl.when((j >= J_TAIL) & (j < J_U))
    def _():
        tiles(wt_ref, lambda r, sl: _sigmoid(r))

    @pl.when(j == J_U)
    def _():
        tiles(wt_ref, lambda r, sl: r)


def _inproj(hn, w_in, w_tail, conv_w, conv_b, seqlen, tm=1024):
    t = hn.shape[0]
    n_conv = N_XBC // IN_TILE
    n_tail = (N_MAIN - N_DIRECT) // IN_TILE
    return pl.pallas_call(
        functools.partial(_inproj_body, tm=tm, tiles_per_seq=seqlen // tm),
        grid=(N_MAIN // IN_TILE, t // tm),
        in_specs=[
            pl.BlockSpec((tm, D_MODEL), lambda j, i: (i, 0)),
            pl.BlockSpec((D_MODEL, IN_TILE), lambda j, i: (0, jnp.minimum(j, J_TAIL - 1))),
            pl.BlockSpec((D_MODEL, IN_TILE), lambda j, i: (0, jnp.clip(j - J_TAIL, 0, n_tail - 1))),
            pl.BlockSpec((CONV_A, IN_TILE), lambda j, i: (0, jnp.clip(j - J_XS, 0, n_conv - 1))),
            pl.BlockSpec((1, IN_TILE), lambda j, i: (0, jnp.clip(j - J_XS, 0, n_conv - 1))),
        ],
        out_specs=pl.BlockSpec((tm, IN_TILE), lambda j, i: (i, j)),
        out_shape=jax.ShapeDtypeStruct((t, N_MAIN), BF16),
        scratch_shapes=[
            pltpu.VMEM((D_MODEL, IN_TILE), BF16),
            pltpu.VMEM((tm + SUBLANES, IN_TILE), F32),
            pltpu.VMEM((SUBLANES, IN_TILE), F32),
        ],
        compiler_params=_cparams(2),
        name="inproj",
    )(hn, w_in, w_tail, conv_w, conv_b)


def _mamba_body(sz_ref, xs_ref, b_ref, c_ref, sga_ref, dtraw_ref,
                dtb_ref, aneg_ref, dexp_ref, e_ref, nw_ref, wproj_ref,
                o_ref,
                xg, bg, cg, zg, st_scr, yn_scr, csg, cstg, dtbf, ecsbf, dstbf, ecsl):
    q = CHUNK

    @pl.when(pl.program_id(1) == 0)
    def _():
        st_scr[...] = jnp.zeros_like(st_scr)

    for g in range(N_GROUPS):
        xg[g] = xs_ref[:, g * GROUP_CH:(g + 1) * GROUP_CH]
        bg[g] = b_ref[:, g * D_STATE:(g + 1) * D_STATE]
        cg[g] = c_ref[:, g * D_STATE:(g + 1) * D_STATE]
        zg[g] = sz_ref[:, g * GROUP_CH:(g + 1) * GROUP_CH]

    dt = _softplus(dtraw_ref[...] + dtb_ref[...])
    a = aneg_ref[...] * dt
    row = lax.broadcasted_iota(jnp.int32, (q, q), 0)
    col = lax.broadcasted_iota(jnp.int32, (q, q), 1)
    causal = row >= col
    tril = jnp.where(causal, 1.0, 0.0).astype(BF16)
    ah, am, al = _split3(a)
    cs = _dot(tril, ah) + _dot(tril, am) + _dot(tril, al)
    ecs = jnp.exp(cs)
    dtbf[...] = dt.astype(BF16)
    ecsbf[...] = ecs.astype(BF16)
    dstbf[...] = jnp.exp(cs[q - 1:q, :] - cs).astype(BF16)
    ecsl[...] = jnp.broadcast_to(ecs[q - 1:q, :], (SUBLANES, LANES))
    for g in range(N_GROUPS):
        cs_g = cs if g == 0 else pltpu.roll(cs, LANES - DT_LANE_STRIDE * g, axis=1)
        csg[g] = cs_g
        cstg[g] = cs_g.T

    lane = lax.broadcasted_iota(jnp.int32, (q, LANES), 1)
    first_half = lane < HEAD_DIM

    def group_body(g, carry):
        xs_g = xg[g].astype(F32)
        b_bf = bg[g]
        c_bf = cg[g]
        e_g = e_ref[g]
        dtx = _dot(dtbf[...], e_g)
        ecsx = _dot(ecsbf[...], e_g)
        dstx = _dot(dstbf[...], e_g)
        sdec = _dot3(ecsl[...], e_g)[0:1, :]
        xdt = xs_g * dtx
        x_bf = xdt.astype(BF16)
        xd_bf = (xdt * dstx).astype(BF16)
        cb = lax.dot_general(c_bf, b_bf, (((1,), (1,)), ((), ())), preferred_element_type=F32)
        cs_cols = csg[g]
        cs_rows = cstg[g]
        pieces = []
        for p in range(HEADS_PER_GROUP // 2):
            xp_ = x_bf[:, p * LANES:(p + 1) * LANES]
            acc = None
            for k in range(2):
                r = 2 * p + k
                diff = cs_cols[:, r:r + 1] - cs_rows[r:r + 1, :]
                m = (jnp.exp(jnp.where(causal, diff, -jnp.inf)) * cb).astype(BF16)
                keep = first_half if k == 0 else jnp.logical_not(first_half)
                xk = jnp.where(keep, xp_, jnp.zeros_like(xp_))
                t = _dot(m, xk)
                acc = t if acc is None else acc + t
            pieces.append(acc)
        y = jnp.concatenate(pieces, axis=1)
        s_old = st_scr[g]
        y = y + _dot(c_bf, s_old.astype(BF16)) * ecsx
        st_scr[g] = sdec * s_old + lax.dot_general(
            b_bf, xd_bf, (((0,), (0,)), ((), ())), preferred_element_type=F32)
        y = y + dexp_ref[g] * xs_g
        y = y * zg[g].astype(F32)
        ms = jnp.mean(y * y, axis=-1, keepdims=True)
        yn_scr[g] = ((y * lax.rsqrt(ms + EPS)) * nw_ref[g]).astype(BF16)
        return carry

    lax.fori_loop(0, N_GROUPS, group_body, 0)

    yn = jnp.concatenate([yn_scr[g] for g in range(N_GROUPS)], axis=1)
    ya = _dot(yn, wproj_ref[...])
    o_ref[...] = (ya * sga_ref[...].astype(F32)).astype(o_ref.dtype)


def _mamba(proj, dt_raw, p, batch, seqlen):
    q = CHUNK
    nc = seqlen // q
    t = batch * seqlen

    def rows(b, c):
        return b * nc + c

    def colblk(width, off):
        return pl.BlockSpec((q, width), lambda b, c: (rows(b, c), off // width))

    def full(shape):
        return pl.BlockSpec(shape, lambda b, c: (0,) * len(shape))

    return pl.pallas_call(
        _mamba_body,
        grid=(batch, nc),
        in_specs=[
            colblk(D_INNER, COL_Z), colblk(D_INNER, COL_XS),
            colblk(N_GROUPS * D_STATE, COL_B), colblk(N_GROUPS * D_STATE, COL_C),
            colblk(D_MODEL, COL_GA),
            pl.BlockSpec((q, LANES), lambda b, c: (rows(b, c), 0)),
            full((1, LANES)), full((1, LANES)),
            full((N_GROUPS, 1, GROUP_CH)), full((N_GROUPS, LANES, GROUP_CH)),
            full((N_GROUPS, 1, GROUP_CH)), full((D_INNER, D_MODEL)),
        ],
        out_specs=pl.BlockSpec((q, D_MODEL), lambda b, c: (rows(b, c), 0)),
        out_shape=jax.ShapeDtypeStruct((t, D_MODEL), BF16),
        scratch_shapes=[
            pltpu.VMEM((N_GROUPS, q, GROUP_CH), BF16),
            pltpu.VMEM((N_GROUPS, q, D_STATE), BF16),
            pltpu.VMEM((N_GROUPS, q, D_STATE), BF16),
            pltpu.VMEM((N_GROUPS, q, GROUP_CH), BF16),
            pltpu.VMEM((N_GROUPS, D_STATE, GROUP_CH), F32),
            pltpu.VMEM((N_GROUPS, q, GROUP_CH), BF16),
            pltpu.VMEM((N_GROUPS, q, LANES), F32),
            pltpu.VMEM((N_GROUPS, LANES, q), F32),
            pltpu.VMEM((q, LANES), BF16),
            pltpu.VMEM((q, LANES), BF16),
            pltpu.VMEM((q, LANES), BF16),
            pltpu.VMEM((SUBLANES, LANES), F32),
        ],
        compiler_params=_cparams(2),
        name="mamba",
    )(proj, proj, proj, proj, proj, dt_raw,
      p["dtb"], p["aneg"], p["dexp"], p["e"], p["nw"], p["wproj"])


S5_BLOCKS = D_S5 // LANES
S5_SLABS = 2 * (S5_NG // S5_BLOCKS) * S5_STATE // LANES
S5_HALF = S5_SLABS // 2


def _s5_body(u_ref, gb_ref, ya_ref, wb_ref, wc_ref, lr_ref, li_ref, d_ref, wglu_ref,
             o_ref, xs_scr, st_scr, y_scr, *, tc):
    @pl.when(pl.program_id(1) == 0)
    def _():
        st_scr[...] = jnp.zeros_like(st_scr)

    u = u_ref[...]
    for s in range(S5_BLOCKS):
        bu = _dot(u[:, s * LANES:(s + 1) * LANES], wb_ref[s])
        for k in range(S5_SLABS):
            xs_scr[k, pl.ds(s, tc, stride=SUBLANES), :] = bu[:, k * LANES:(k + 1) * LANES]

    lr = [lr_ref[k] for k in range(S5_HALF)]
    li = [li_ref[k] for k in range(S5_HALF)]

    def step(t, st):
        r0 = pl.multiple_of(t * SUBLANES, SUBLANES)
        new_re, new_im = [], []
        for k in range(S5_HALF):
            re, im = st[k], st[S5_HALF + k]
            nre = (lr[k] * re - li[k] * im) + xs_scr[k, pl.ds(r0, SUBLANES), :]
            nim = (lr[k] * im + li[k] * re) + xs_scr[S5_HALF + k, pl.ds(r0, SUBLANES), :]
            xs_scr[k, pl.ds(r0, SUBLANES), :] = nre
            xs_scr[S5_HALF + k, pl.ds(r0, SUBLANES), :] = nim
            new_re.append(nre)
            new_im.append(nim)
        return tuple(new_re + new_im)

    st0 = tuple(st_scr[k] for k in range(S5_SLABS))
    st = lax.fori_loop(0, tc, step, st0, unroll=8)
    for k in range(S5_SLABS):
        st_scr[k] = st[k]

    for s in range(S5_BLOCKS):
        xb = jnp.concatenate(
            [xs_scr[k, pl.ds(s, tc, stride=SUBLANES), :] for k in range(S5_SLABS)], axis=1)
        y_scr[:, s * LANES:(s + 1) * LANES] = _dot(xb.astype(BF16), wc_ref[s])

    y = y_scr[...] + d_ref[...] * u.astype(F32)
    y = _gelu_tanh(y).astype(BF16)
    vg = _dot(y, wglu_ref[...])
    yb = vg[:, :D_MODEL] * _sigmoid(vg[:, D_MODEL:])
    merged = ya_ref[...].astype(F32) + gb_ref[...].astype(F32) * yb
    o_ref[...] = merged.astype(o_ref.dtype)


def _s5(proj, ya_gated, p, batch, seqlen, tc=256):
    nc = seqlen // tc
    t = batch * seqlen

    def rows(b, c):
        return b * nc + c

    def full(shape):
        return pl.BlockSpec(shape, lambda b, c: (0,) * len(shape))

    return pl.pallas_call(
        functools.partial(_s5_body, tc=tc),
        grid=(batch, nc),
        in_specs=[
            pl.BlockSpec((tc, D_S5), lambda b, c: (rows(b, c), COL_U // D_S5)),
            pl.BlockSpec((tc, D_MODEL), lambda b, c: (rows(b, c), COL_GB // D_MODEL)),
            pl.BlockSpec((tc, D_MODEL), lambda b, c: (rows(b, c), 0)),
            full((S5_BLOCKS, LANES, S5_SLABS * LANES)),
            full((S5_BLOCKS, S5_SLABS * LANES, LANES)),
            full((S5_HALF, SUBLANES, LANES)), full((S5_HALF, SUBLANES, LANES)),
            full((1, D_S5)), full((D_S5, 2 * D_MODEL)),
        ],
        out_specs=pl.BlockSpec((tc, D_MODEL), lambda b, c: (rows(b, c), 0)),
        out_shape=jax.ShapeDtypeStruct((t, D_MODEL), BF16),
        scratch_shapes=[
            pltpu.VMEM((S5_SLABS, tc * SUBLANES, LANES), F32),
            pltpu.VMEM((S5_SLABS, SUBLANES, LANES), F32),
            pltpu.VMEM((tc, D_S5), F32),
        ],
        compiler_params=_cparams(2),
        name="s5",
    )(proj, proj, ya_gated, p["wb"], p["wc"], p["lr"], p["li"], p["d"], p["wglu"])


def _merge_body(x_ref, m_ref, w_ref, o_ref):
    o_ref[...] = x_ref[...] + _dot(m_ref[...], w_ref[...])


def _merge_proj(x2, merged, w_out, tm=512):
    t = x2.shape[0]
    return pl.pallas_call(
        _merge_body,
        grid=(t // tm,),
        in_specs=[
            pl.BlockSpec((tm, D_MODEL), lambda i: (i, 0)),
            pl.BlockSpec((tm, D_MODEL), lambda i: (i, 0)),
            pl.BlockSpec((D_MODEL, D_MODEL), lambda i: (0, 0)),
        ],
        out_specs=pl.BlockSpec((tm, D_MODEL), lambda i: (i, 0)),
        out_shape=jax.ShapeDtypeStruct((t, D_MODEL), F32),
        compiler_params=_cparams(1),
        name="merge_proj",
    )(x2, merged, w_out)


def _ffn_body(h_ref, halo_ref, nw_ref, wg_ref, wv_ref, cwg_ref, cwv_ref, cbg_ref, cbv_ref,
              wd_ref, fw_ref, o_ref, lhs_scr, upg_scr, upv_scr, acc_scr, *, tm, tiles_per_seq):
    i = pl.program_id(0)
    j = pl.program_id(1)
    halo = BF16_ROWS

    def norm(v):
        ms = jnp.mean(v * v, axis=-1, keepdims=True)
        return ((v * lax.rsqrt(ms + EPS)) * nw_ref[...]).astype(BF16)

    @pl.when(j == 0)
    def _():
        lhs_scr[halo:halo + tm, :] = norm(h_ref[...])
        first = (i % tiles_per_seq) == 0
        hn_halo = norm(halo_ref[...])
        lhs_scr[0:halo, :] = jnp.where(first, jnp.zeros_like(hn_halo), hn_halo)
        acc_scr[...] = jnp.zeros_like(acc_scr)

    lhs = lhs_scr[...]
    upg_scr[...] = _dot(lhs, wg_ref[...])
    upv_scr[...] = _dot(lhs, wv_ref[...])

    def conv(up_scr, cw_ref, cb_ref):
        acc = cb_ref[...] + cw_ref[2:3, :] * up_scr[pl.ds(halo, tm), :]
        acc = acc + cw_ref[1:2, :] * up_scr[pl.ds(halo - 1, tm), :]
        acc = acc + cw_ref[0:1, :] * up_scr[pl.ds(halo - 2, tm), :]
        return acc

    act = (_silu(conv(upg_scr, cwg_ref, cbg_ref)) * conv(upv_scr, cwv_ref, cbv_ref)).astype(BF16)
    acc_scr[...] += _dot(act, wd_ref[...])

    @pl.when(j == pl.num_programs(1) - 1)
    def _():
        h2 = h_ref[...] + acc_scr[...]
        ms = jnp.mean(h2 * h2, axis=-1, keepdims=True)
        o_ref[...] = (h2 * lax.rsqrt(ms + EPS)) * fw_ref[...]


def _ffn(h1, norm_w, w_up, conv_w, conv_b, w_down, final_w, seqlen, tm=512, tn=512):
    t = h1.shape[0]
    nj = D_FF // tn
    halo = BF16_ROWS
    halo_blocks = tm // halo
    return pl.pallas_call(
        functools.partial(_ffn_body, tm=tm, tiles_per_seq=seqlen // tm),
        grid=(t // tm, nj),
        in_specs=[
            pl.BlockSpec((tm, D_MODEL), lambda i, j: (i, 0)),
            pl.BlockSpec((halo, D_MODEL), lambda i, j: (jnp.maximum(i * halo_blocks - 1, 0), 0)),
            pl.BlockSpec((1, D_MODEL), lambda i, j: (0, 0)),
            pl.BlockSpec((D_MODEL, tn), lambda i, j: (0, j)),
            pl.BlockSpec((D_MODEL, tn), lambda i, j: (0, nj + j)),
            pl.BlockSpec((CONV_FFN, tn), lambda i, j: (0, j)),
            pl.BlockSpec((CONV_FFN, tn), lambda i, j: (0, nj + j)),
            pl.BlockSpec((1, tn), lambda i, j: (0, j)),
            pl.BlockSpec((1, tn), lambda i, j: (0, nj + j)),
            pl.BlockSpec((tn, D_MODEL), lambda i, j: (j, 0)),
            pl.BlockSpec((1, D_MODEL), lambda i, j: (0, 0)),
        ],
        out_specs=pl.BlockSpec((tm, D_MODEL), lambda i, j: (i, 0)),
        out_shape=jax.ShapeDtypeStruct((t, D_MODEL), F32),
        scratch_shapes=[
            pltpu.VMEM((tm + halo, D_MODEL), BF16),
            pltpu.VMEM((tm + halo, tn), F32),
            pltpu.VMEM((tm + halo, tn), F32),
            pltpu.VMEM((tm, D_MODEL), F32),
        ],
        compiler_params=_cparams(2),
        name="ffn",
    )(h1, h1, norm_w, w_up, w_up, conv_w, conv_w, conv_b, conv_b, w_down, final_w)


def _mamba_params(dt_bias, a_log, d_a, norm_w, w_proj):
    def grouped(v, width):
        lead = v.shape[:-1]
        v = v.reshape(lead + (N_GROUPS, width))
        return jnp.moveaxis(v, -2, 0)

    def head_lanes(v):
        v = v.reshape(N_GROUPS, HEADS_PER_GROUP)
        v = jnp.pad(v, ((0, 0), (0, DT_LANE_STRIDE - HEADS_PER_GROUP)))
        return v.reshape(1, LANES)

    lane_head = jnp.arange(LANES)
    ch_head = jnp.arange(GROUP_CH) // HEAD_DIM
    e = []
    for g in range(N_GROUPS):
        e.append((lane_head[:, None] == (DT_LANE_STRIDE * g + ch_head)[None, :]))
    e = jnp.stack(e).astype(BF16)
    return {
        "dtb": head_lanes(dt_bias),
        "aneg": head_lanes(-jnp.exp(a_log)),
        "dexp": grouped(jnp.repeat(d_a, HEAD_DIM)[None, :], GROUP_CH),
        "e": e,
        "nw": grouped(norm_w[None, :], GROUP_CH),
        "wproj": w_proj.astype(BF16),
    }


def _s5_params(lam_re, lam_im, log_dt, b_re, b_im, c_re, c_im, d_skip, w_glu):
    lre = jnp.minimum(lam_re, EIG_MAX)
    lim = lam_im
    dt = jnp.exp(log_dt)[:, None]
    mag = jnp.exp(lre * dt)
    lbr = mag * jnp.cos(lim * dt)
    lbi = mag * jnp.sin(lim * dt)
    nr, ni = lbr - 1.0, lbi
    den = lre * lre + lim * lim
    cr = (nr * lre + ni * lim) / den
    ci = (ni * lre - nr * lim) / den
    bbr = cr[..., None] * b_re - ci[..., None] * b_im
    bbi = cr[..., None] * b_im + ci[..., None] * b_re
    gl = S5_NG // S5_BLOCKS
    eye = jnp.eye(gl, dtype=F32)

    def in_map(v):
        v = v.reshape(S5_BLOCKS, gl, S5_STATE, S5_GROUP).transpose(0, 1, 3, 2)
        v = jnp.einsum("sgcp,gh->sgchp", v, eye)
        return v.reshape(S5_BLOCKS, gl * S5_GROUP, gl * S5_STATE)

    def out_map(v):
        v = v.reshape(S5_BLOCKS, gl, S5_GROUP, S5_STATE).transpose(0, 1, 3, 2)
        v = jnp.einsum("sgpc,gh->sgphc", v, eye)
        return v.reshape(S5_BLOCKS, gl * S5_STATE, gl * S5_GROUP)

    def scan_coef(v):
        return v.reshape(S5_BLOCKS, S5_HALF, LANES).transpose(1, 0, 2)

    return {
        "wb": jnp.concatenate([in_map(bbr), in_map(bbi)], axis=2).astype(BF16),
        "wc": jnp.concatenate([out_map(c_re), -out_map(c_im)], axis=1).astype(BF16),
        "lr": scan_coef(lbr),
        "li": scan_coef(lbi),
        "d": d_skip[None, :],
        "wglu": w_glu.astype(BF16),
    }


def _inproj_weights(w_in):
    gn = N_GROUPS * D_STATE
    o_xbc = D_INNER
    o_dt = o_xbc + D_INNER + 2 * gn
    o_u = o_dt + N_HEADS
    o_ga = o_u + D_S5
    o_gb = o_ga + D_MODEL
    seg = lambda a, n: w_in[:, a:a + n]
    w_tail = jnp.concatenate([seg(o_ga, D_MODEL), seg(o_gb, D_MODEL), seg(o_u, D_S5)],
                             axis=1).astype(BF16)
    w_dt = seg(o_dt, N_HEADS).reshape(D_MODEL, N_GROUPS, HEADS_PER_GROUP)
    w_dt = jnp.pad(w_dt, ((0, 0), (0, 0), (0, DT_LANE_STRIDE - HEADS_PER_GROUP)))
    return w_tail, w_dt.reshape(D_MODEL, LANES).astype(BF16)


def kernel(x, norm_mix_w, w_in, conv_a_w, conv_a_b, dt_bias, a_log, d_a, norm_a_w, w_proj_a,
           s5_lam_re, s5_lam_im, s5_log_dt, s5_b_re, s5_b_im, s5_c_re, s5_c_im, s5_d, w_s5_glu,
           w_out, norm_ffn_w, w_up, conv_ffn_w, conv_ffn_b, w_down, norm_final_w):
    batch, seqlen, _ = x.shape
    assert w_in.shape[0] == 1, "single-layer block: the final norm is fused into the FFN stage"
    h = x.reshape(batch * seqlen, D_MODEL)
    w_tail, w_dt = _inproj_weights(w_in[0])
    hn, dt_raw = _norm_dt(h, norm_mix_w[0][None, :], w_dt)
    proj = _inproj(hn, w_in[0], w_tail, conv_a_w[0], conv_a_b[0][None, :], seqlen)
    mp = _mamba_params(dt_bias[0], a_log[0], d_a[0], norm_a_w[0], w_proj_a[0])
    ya_gated = _mamba(proj, dt_raw, mp, batch, seqlen)
    sp = _s5_params(s5_lam_re[0], s5_lam_im[0], s5_log_dt[0], s5_b_re[0], s5_b_im[0],
                    s5_c_re[0], s5_c_im[0], s5_d[0], w_s5_glu[0])
    merged = _s5(proj, ya_gated, sp, batch, seqlen)
    h1 = _merge_proj(h, merged, w_out[0].astype(BF16))
    out = _ffn(h1, norm_ffn_w[0][None, :], w_up[0].astype(BF16), conv_ffn_w[0],
               conv_ffn_b[0][None, :], w_down[0].astype(BF16), norm_final_w[None, :], seqlen)
    return out.reshape(batch, seqlen, D_MODEL)
```

```python
import functools
import math

import jax
import jax.numpy as jnp
from jax import lax
from jax.experimental import pallas as pl
from jax.experimental.pallas import tpu as pltpu

F32 = jnp.float32
BF16 = jnp.bfloat16

D_MODEL = 2048
D_INNER = 3072
HEAD_DIM = 64
N_HEADS = 48
N_GROUPS = 8
HEADS_PER_GROUP = 6
GROUP_CH = D_INNER // N_GROUPS
D_STATE = 128
CONV_A = 4
CHUNK = 256
D_S5 = 1024
S5_GROUP = 16
S5_NG = 64
S5_STATE = 64
D_FF = 5632
CONV_FFN = 3
EPS = 1e-6
EIG_MAX = -1e-4

LANES = 128
SUBLANES = 8
BF16_ROWS = 16
VMEM_LIMIT = 56 * 1024 * 1024

COL_Z = 0
COL_XS = 3072
COL_B = 6144
COL_C = 7168
COL_GA = 8192
COL_GB = 10240
COL_U = 12288
N_MAIN = 13312
N_XBC = D_INNER + 2 * N_GROUPS * D_STATE
N_DIRECT = D_INNER + N_XBC
DT_LANE_STRIDE = 16
NEG_LOG2E = -1.4426950408889634


def _sigmoid(v):
    return 1.0 / (1.0 + jnp.exp2(v * NEG_LOG2E))


def _silu(v):
    return v * _sigmoid(v)


def _softplus(v):
    return jnp.maximum(v, 0.0) + jnp.log1p(jnp.exp(-jnp.abs(v)))


def _gelu_tanh(v):
    c = math.sqrt(2.0 / math.pi)
    return 0.5 * v * (1.0 + jnp.tanh(c * (v + 0.044715 * (v * v * v))))


def _split3(v):
    hi = v.astype(BF16)
    r1 = v - hi.astype(F32)
    mid = r1.astype(BF16)
    lo = (r1 - mid.astype(F32)).astype(BF16)
    return hi, mid, lo


def _dot(a, b):
    return jnp.dot(a, b, preferred_element_type=F32)


def _dot3(v, m_bf16):
    hi, mid, lo = _split3(v)
    return _dot(hi, m_bf16) + _dot(mid, m_bf16) + _dot(lo, m_bf16)


def _cparams(n_axes):
    return pltpu.CompilerParams(dimension_semantics=("arbitrary",) * n_axes,
                                vmem_limit_bytes=VMEM_LIMIT)


def _norm_body(x_ref, nw_ref, wdt_ref, hn_ref, dt_ref):
    x = x_ref[...]
    ms = jnp.mean(x * x, axis=-1, keepdims=True)
    hn = ((x * lax.rsqrt(ms + EPS)) * nw_ref[...]).astype(BF16)
    hn_ref[...] = hn
    dt_ref[...] = _dot(hn, wdt_ref[...])


def _norm_dt(x2, norm_w, w_dt, tm=512):
    t = x2.shape[0]
    return pl.pallas_call(
        _norm_body,
        grid=(t // tm,),
        in_specs=[
            pl.BlockSpec((tm, D_MODEL), lambda i: (i, 0)),
            pl.BlockSpec((1, D_MODEL), lambda i: (0, 0)),
            pl.BlockSpec((D_MODEL, LANES), lambda i: (0, 0)),
        ],
        out_specs=[
            pl.BlockSpec((tm, D_MODEL), lambda i: (i, 0)),
            pl.BlockSpec((tm, LANES), lambda i: (i, 0)),
        ],
        out_shape=[
            jax.ShapeDtypeStruct((t, D_MODEL), BF16),
            jax.ShapeDtypeStruct((t, LANES), F32),
        ],
        compiler_params=_cparams(1),
        name="norm_dt",
    )(x2, norm_w, w_dt)


IN_TILE = 1024
IN_SUB = 256
J_XS = COL_XS // IN_TILE
J_GA = COL_GA // IN_TILE
J_U = COL_U // IN_TILE
ROW_U = N_DIRECT + N_HEADS
ROW_GA = ROW_U + D_S5


def _inproj_row(j):
    assert ROW_GA % SUBLANES == 0 and ROW_U % SUBLANES == 0
    tile8 = IN_TILE // SUBLANES
    row8 = jnp.where(j < J_GA, j * tile8,
                     jnp.where(j < J_U, ROW_GA // SUBLANES + (j - J_GA) * tile8, ROW_U // SUBLANES))
    return row8 * SUBLANES


def _inproj_body(hn_ref, wt_ref, cw_ref, cb_ref, o_ref, wbf, xp, carry, *, tm, tiles_per_seq):
    j = pl.program_id(0)
    i = pl.program_id(1)
    halo = SUBLANES

    @pl.when(i == 0)
    def _():
        for n in range(IN_TILE // IN_SUB):
            sl = slice(n * IN_SUB, (n + 1) * IN_SUB)
            wbf[:, sl] = wt_ref[sl, :].T.astype(BF16)

    def tiles(epilogue):
        hn = hn_ref[...]
        for n in range(IN_TILE // IN_SUB):
            sl = slice(n * IN_SUB, (n + 1) * IN_SUB)
            r = _dot(hn, wbf[:, sl])
            o_ref[:, sl] = epilogue(r, sl).astype(o_ref.dtype)

    @pl.when(j < J_XS)
    def _():
        tiles(lambda r, sl: _silu(r))

    @pl.when((j >= J_XS) & (j < J_GA))
    def _():
        first = (i % tiles_per_seq) == 0

        def conv_silu(r, sl):
            prev = carry[:, sl]
            xp[0:halo, sl] = jnp.where(first, jnp.zeros_like(prev), prev)
            xp[halo:halo + tm, sl] = r
            carry[:, sl] = r[tm - halo:tm, :]
            acc = cb_ref[:, sl] + cw_ref[3:4, sl] * r
            acc = acc + cw_ref[2:3, sl] * xp[pl.ds(halo - 1, tm), sl]
            acc = acc + cw_ref[1:2, sl] * xp[pl.ds(halo - 2, tm), sl]
            acc = acc + cw_ref[0:1, sl] * xp[pl.ds(halo - 3, tm), sl]
            return _silu(acc)

        tiles(conv_silu)

    @pl.when((j >= J_GA) & (j < J_U))
    def _():
        tiles(lambda r, sl: _sigmoid(r))

    @pl.when(j == J_U)
    def _():
        tiles(lambda r, sl: r)


def _inproj(hn, w_t, conv_w, conv_b, seqlen, tm=1024):
    t = hn.shape[0]
    n_conv = N_XBC // IN_TILE
    return pl.pallas_call(
        functools.partial(_inproj_body, tm=tm, tiles_per_seq=seqlen // tm),
        grid=(N_MAIN // IN_TILE, t // tm),
        in_specs=[
            pl.BlockSpec((tm, D_MODEL), lambda j, i: (i, 0)),
            pl.BlockSpec((pl.Element(IN_TILE), pl.Element(D_MODEL)),
                         lambda j, i: (_inproj_row(j), 0)),
            pl.BlockSpec((CONV_A, IN_TILE), lambda j, i: (0, jnp.clip(j - J_XS, 0, n_conv - 1))),
            pl.BlockSpec((1, IN_TILE), lambda j, i: (0, jnp.clip(j - J_XS, 0, n_conv - 1))),
        ],
        out_specs=pl.BlockSpec((tm, IN_TILE), lambda j, i: (i, j)),
        out_shape=jax.ShapeDtypeStruct((t, N_MAIN), BF16),
        scratch_shapes=[
            pltpu.VMEM((D_MODEL, IN_TILE), BF16),
            pltpu.VMEM((tm + SUBLANES, IN_TILE), F32),
            pltpu.VMEM((SUBLANES, IN_TILE), F32),
        ],
        compiler_params=_cparams(2),
        name="inproj",
    )(hn, w_t, conv_w, conv_b)


def _mamba_body(sz_ref, xs_ref, b_ref, c_ref, sga_ref, dtraw_ref,
                dtb_ref, aneg_ref, dexp_ref, e_ref, nw_ref, wproj_ref,
                o_ref,
                xg, bg, cg, zg, st_scr, acc_scr, csg, cstg, dtbf, ecsbf, dstbf, ecsl):
    q = CHUNK

    @pl.when(pl.program_id(1) == 0)
    def _():
        st_scr[...] = jnp.zeros_like(st_scr)

    for g in range(N_GROUPS):
        xg[g] = xs_ref[:, g * GROUP_CH:(g + 1) * GROUP_CH]
        bg[g] = b_ref[:, g * D_STATE:(g + 1) * D_STATE]
        cg[g] = c_ref[:, g * D_STATE:(g + 1) * D_STATE]
        zg[g] = sz_ref[:, g * GROUP_CH:(g + 1) * GROUP_CH]

    dt = _softplus(dtraw_ref[...] + dtb_ref[...])
    a = aneg_ref[...] * dt
    row = lax.broadcasted_iota(jnp.int32, (q, q), 0)
    col = lax.broadcasted_iota(jnp.int32, (q, q), 1)
    causal = row >= col
    tril = jnp.where(causal, 1.0, 0.0).astype(BF16)
    ah, am, al = _split3(a)
    cs = _dot(tril, ah) + _dot(tril, am) + _dot(tril, al)
    ecs = jnp.exp(cs)
    dtbf[...] = dt.astype(BF16)
    ecsbf[...] = ecs.astype(BF16)
    dstbf[...] = jnp.exp(cs[q - 1:q, :] - cs).astype(BF16)
    ecsl[...] = jnp.broadcast_to(ecs[q - 1:q, :], (SUBLANES, LANES))
    for g in range(N_GROUPS):
        cs_g = cs if g == 0 else pltpu.roll(cs, LANES - DT_LANE_STRIDE * g, axis=1)
        csg[g] = cs_g
        cstg[g] = cs_g.T

    lane = lax.broadcasted_iota(jnp.int32, (q, LANES), 1)
    first_half = lane < HEAD_DIM

    hq = q // 2
    causal_top = causal[0:hq, 0:hq]
    causal_bot = causal[hq:q, :]
    nt = (((1,), (1,)), ((), ()))
    tn = (((0,), (0,)), ((), ()))

    def one_group(g):
        xs_g = xg[g].astype(F32)
        b_bf = bg[g]
        c_bf = cg[g]
        e_g = e_ref[g]
        dtx = _dot(dtbf[...], e_g)
        ecsx = _dot(ecsbf[...], e_g)
        dstx = _dot(dstbf[...], e_g)
        sdec = _dot3(ecsl[...], e_g)[0:1, :]
        xdt = xs_g * dtx
        x_bf = xdt.astype(BF16)
        xd_bf = (xdt * dstx).astype(BF16)
        cb_top = lax.dot_general(c_bf[0:hq], b_bf[0:hq], nt, preferred_element_type=F32)
        cb_bot = lax.dot_general(c_bf[hq:q], b_bf, nt, preferred_element_type=F32)
        cs_cols = csg[g]
        cs_rows = cstg[g]
        pieces = []
        for p in range(HEADS_PER_GROUP // 2):
            xp_ = x_bf[:, p * LANES:(p + 1) * LANES]
            acc = None
            for k in range(2):
                r = 2 * p + k
                d_top = cs_cols[0:hq, r:r + 1] - cs_rows[r:r + 1, 0:hq]
                d_bot = cs_cols[hq:q, r:r + 1] - cs_rows[r:r + 1, :]
                m_top = (jnp.exp(jnp.where(causal_top, d_top, -jnp.inf)) * cb_top).astype(BF16)
                m_bot = (jnp.exp(jnp.where(causal_bot, d_bot, -jnp.inf)) * cb_bot).astype(BF16)
                keep = first_half if k == 0 else jnp.logical_not(first_half)
                xk = jnp.where(keep, xp_, jnp.zeros_like(xp_))
                t = jnp.concatenate([_dot(m_top, xk[0:hq]), _dot(m_bot, xk)], axis=0)
                acc = t if acc is None else acc + t
            pieces.append(acc)
        y = jnp.concatenate(pieces, axis=1)
        s_old = st_scr[g]
        y = y + _dot(c_bf, s_old.astype(BF16)) * ecsx
        st_scr[g] = sdec * s_old + lax.dot_general(b_bf, xd_bf, tn, preferred_element_type=F32)
        y = y + dexp_ref[g] * xs_g
        y = y * zg[g].astype(F32)
        ms = jnp.mean(y * y, axis=-1, keepdims=True)
        return ((y * lax.rsqrt(ms + EPS)) * nw_ref[g]).astype(BF16)

    def pair_body(p, carry):
        yn = jnp.concatenate([one_group(2 * p), one_group(2 * p + 1)], axis=1)
        acc_scr[...] += _dot(yn, wproj_ref[p])
        return carry

    acc_scr[...] = jnp.zeros_like(acc_scr)
    lax.fori_loop(0, N_GROUPS // 2, pair_body, 0)
    o_ref[...] = (acc_scr[...] * sga_ref[...].astype(F32)).astype(o_ref.dtype)


def _mamba(proj, dt_raw, p, batch, seqlen):
    q = CHUNK
    nc = seqlen // q
    t = batch * seqlen

    def rows(b, c):
        return b * nc + c

    def colblk(width, off):
        return pl.BlockSpec((q, width), lambda b, c: (rows(b, c), off // width))

    def full(shape):
        return pl.BlockSpec(shape, lambda b, c: (0,) * len(shape))

    return pl.pallas_call(
        _mamba_body,
        grid=(batch, nc),
        in_specs=[
            colblk(D_INNER, COL_Z), colblk(D_INNER, COL_XS),
            colblk(N_GROUPS * D_STATE, COL_B), colblk(N_GROUPS * D_STATE, COL_C),
            colblk(D_MODEL, COL_GA),
            pl.BlockSpec((q, LANES), lambda b, c: (rows(b, c), 0)),
            full((1, LANES)), full((1, LANES)),
            full((N_GROUPS, 1, GROUP_CH)), full((N_GROUPS, LANES, GROUP_CH)),
            full((N_GROUPS, 1, GROUP_CH)), full((N_GROUPS // 2, 2 * GROUP_CH, D_MODEL)),
        ],
        out_specs=pl.BlockSpec((q, D_MODEL), lambda b, c: (rows(b, c), 0)),
        out_shape=jax.ShapeDtypeStruct((t, D_MODEL), BF16),
        scratch_shapes=[
            pltpu.VMEM((N_GROUPS, q, GROUP_CH), BF16),
            pltpu.VMEM((N_GROUPS, q, D_STATE), BF16),
            pltpu.VMEM((N_GROUPS, q, D_STATE), BF16),
            pltpu.VMEM((N_GROUPS, q, GROUP_CH), BF16),
            pltpu.VMEM((N_GROUPS, D_STATE, GROUP_CH), F32),
            pltpu.VMEM((q, D_MODEL), F32),
            pltpu.VMEM((N_GROUPS, q, LANES), F32),
            pltpu.VMEM((N_GROUPS, LANES, q), F32),
            pltpu.VMEM((q, LANES), BF16),
            pltpu.VMEM((q, LANES), BF16),
            pltpu.VMEM((q, LANES), BF16),
            pltpu.VMEM((SUBLANES, LANES), F32),
        ],
        compiler_params=_cparams(2),
        name="mamba",
    )(proj, proj, proj, proj, proj, dt_raw,
      p["dtb"], p["aneg"], p["dexp"], p["e"], p["nw"], p["wproj"])


S5_BLOCKS = D_S5 // LANES
S5_SLABS = 2 * (S5_NG // S5_BLOCKS) * S5_STATE // LANES
S5_HALF = S5_SLABS // 2
S5_PITCH = 12
S5_UNROLL = 8


def _s5_body(u_ref, gb_ref, ya_ref, wb_ref, wc_ref, lr_ref, li_ref, d_ref, wglu_ref,
             o_ref, xs_scr, st_scr, y_scr, wglu_bf, *, tc):
    @pl.when((pl.program_id(0) == 0) & (pl.program_id(1) == 0))
    def _():
        wglu_bf[...] = wglu_ref[...].astype(BF16)

    @pl.when(pl.program_id(1) == 0)
    def _():
        st_scr[...] = jnp.zeros_like(st_scr)

    u = u_ref[...]
    for s in range(S5_BLOCKS):
        bu = _dot(u[:, s * LANES:(s + 1) * LANES], wb_ref[s])
        for k in range(S5_SLABS):
            xs_scr[k, pl.ds(s, tc, stride=S5_PITCH), :] = bu[:, k * LANES:(k + 1) * LANES]

    lr = [lr_ref[k] for k in range(S5_HALF)]
    li = [li_ref[k] for k in range(S5_HALF)]

    def steps(it, st):
        base = pl.multiple_of(it * (S5_UNROLL * S5_PITCH), SUBLANES)
        for n in range(S5_UNROLL):
            r0 = base + n * S5_PITCH
            new_re, new_im = [], []
            for k in range(S5_HALF):
                re, im = st[k], st[S5_HALF + k]
                nre = (lr[k] * re - li[k] * im) + xs_scr[k, pl.ds(r0, SUBLANES), :]
                nim = (lr[k] * im + li[k] * re) + xs_scr[S5_HALF + k, pl.ds(r0, SUBLANES), :]
                xs_scr[k, pl.ds(r0, SUBLANES), :] = nre
                xs_scr[S5_HALF + k, pl.ds(r0, SUBLANES), :] = nim
                new_re.append(nre)
                new_im.append(nim)
            st = tuple(new_re + new_im)
        return st

    st0 = tuple(st_scr[k] for k in range(S5_SLABS))
    st = lax.fori_loop(0, tc // S5_UNROLL, steps, st0)
    for k in range(S5_SLABS):
        st_scr[k] = st[k]

    for s in range(S5_BLOCKS):
        xb = jnp.concatenate(
            [xs_scr[k, pl.ds(s, tc, stride=S5_PITCH), :] for k in range(S5_SLABS)], axis=1)
        y_scr[:, s * LANES:(s + 1) * LANES] = _dot(xb.astype(BF16), wc_ref[s])

    y = y_scr[...] + d_ref[...] * u.astype(F32)
    y = _gelu_tanh(y).astype(BF16)
    vg = _dot(y, wglu_bf[...])
    yb = vg[:, :D_MODEL] * _sigmoid(vg[:, D_MODEL:])
    merged = ya_ref[...].astype(F32) + gb_ref[...].astype(F32) * yb
    o_ref[...] = merged.astype(o_ref.dtype)


def _s5(proj, ya_gated, p, batch, seqlen, tc=256):
    nc = seqlen // tc
    t = batch * seqlen

    def rows(b, c):
        return b * nc + c

    def full(shape):
        return pl.BlockSpec(shape, lambda b, c: (0,) * len(shape))

    return pl.pallas_call(
        functools.partial(_s5_body, tc=tc),
        grid=(batch, nc),
        in_specs=[
            pl.BlockSpec((tc, D_S5), lambda b, c: (rows(b, c), COL_U // D_S5)),
            pl.BlockSpec((tc, D_MODEL), lambda b, c: (rows(b, c), COL_GB // D_MODEL)),
            pl.BlockSpec((tc, D_MODEL), lambda b, c: (rows(b, c), 0)),
            full((S5_BLOCKS, LANES, S5_SLABS * LANES)),
            full((S5_BLOCKS, S5_SLABS * LANES, LANES)),
            full((S5_HALF, SUBLANES, LANES)), full((S5_HALF, SUBLANES, LANES)),
            full((1, D_S5)),
            pl.BlockSpec((D_S5, 2 * D_MODEL), lambda b, c: (0, 0), pipeline_mode=pl.Buffered(1)),
        ],
        out_specs=pl.BlockSpec((tc, D_MODEL), lambda b, c: (rows(b, c), 0)),
        out_shape=jax.ShapeDtypeStruct((t, D_MODEL), BF16),
        scratch_shapes=[
            pltpu.VMEM((S5_SLABS, tc * S5_PITCH, LANES), F32),
            pltpu.VMEM((S5_SLABS, SUBLANES, LANES), F32),
            pltpu.VMEM((tc, D_S5), F32),
            pltpu.VMEM((D_S5, 2 * D_MODEL), BF16),
        ],
        compiler_params=_cparams(2),
        name="s5",
    )(proj, proj, ya_gated, p["wb"], p["wc"], p["lr"], p["li"], p["d"], p["wglu"])


def _merge_body(x_ref, m_ref, w_ref, o_ref, wbf):
    @pl.when(pl.program_id(0) == 0)
    def _():
        wbf[...] = w_ref[...].astype(BF16)

    o_ref[...] = x_ref[...] + _dot(m_ref[...], wbf[...])


def _merge_proj(x2, merged, w_out, tm=512):
    t = x2.shape[0]
    return pl.pallas_call(
        _merge_body,
        grid=(t // tm,),
        in_specs=[
            pl.BlockSpec((tm, D_MODEL), lambda i: (i, 0)),
            pl.BlockSpec((tm, D_MODEL), lambda i: (i, 0)),
            pl.BlockSpec((D_MODEL, D_MODEL), lambda i: (0, 0), pipeline_mode=pl.Buffered(1)),
        ],
        out_specs=pl.BlockSpec((tm, D_MODEL), lambda i: (i, 0)),
        out_shape=jax.ShapeDtypeStruct((t, D_MODEL), F32),
        scratch_shapes=[pltpu.VMEM((D_MODEL, D_MODEL), BF16)],
        compiler_params=_cparams(1),
        name="merge_proj",
    )(x2, merged, w_out)


def _ffn_body(h_ref, halo_ref, nw_ref, wg_ref, wv_ref, cwg_ref, cwv_ref, cbg_ref, cbv_ref,
              wd_ref, fw_ref, o_ref, lhs_scr, upg_scr, upv_scr, acc_scr, *, tm, tiles_per_seq):
    i = pl.program_id(0)
    j = pl.program_id(1)
    halo = BF16_ROWS

    def norm(v):
        ms = jnp.mean(v * v, axis=-1, keepdims=True)
        return ((v * lax.rsqrt(ms + EPS)) * nw_ref[...]).astype(BF16)

    @pl.when(j == 0)
    def _():
        lhs_scr[halo:halo + tm, :] = norm(h_ref[...])
        first = (i % tiles_per_seq) == 0
        hn_halo = norm(halo_ref[...])
        lhs_scr[0:halo, :] = jnp.where(first, jnp.zeros_like(hn_halo), hn_halo)
        acc_scr[...] = jnp.zeros_like(acc_scr)

    lhs = lhs_scr[...]
    upg_scr[...] = _dot(lhs, wg_ref[...])
    upv_scr[...] = _dot(lhs, wv_ref[...])

    def conv(up_scr, cw_ref, cb_ref):
        acc = cb_ref[...] + cw_ref[2:3, :] * up_scr[pl.ds(halo, tm), :]
        acc = acc + cw_ref[1:2, :] * up_scr[pl.ds(halo - 1, tm), :]
        acc = acc + cw_ref[0:1, :] * up_scr[pl.ds(halo - 2, tm), :]
        return acc

    act = (_silu(conv(upg_scr, cwg_ref, cbg_ref)) * conv(upv_scr, cwv_ref, cbv_ref)).astype(BF16)
    acc_scr[...] += _dot(act, wd_ref[...])

    @pl.when(j == pl.num_programs(1) - 1)
    def _():
        h2 = h_ref[...] + acc_scr[...]
        ms = jnp.mean(h2 * h2, axis=-1, keepdims=True)
        o_ref[...] = (h2 * lax.rsqrt(ms + EPS)) * fw_ref[...]


def _ffn(h1, norm_w, w_up, conv_w, conv_b, w_down, final_w, seqlen, tm=512, tn=512):
    t = h1.shape[0]
    nj = D_FF // tn
    halo = BF16_ROWS
    halo_blocks = tm // halo
    return pl.pallas_call(
        functools.partial(_ffn_body, tm=tm, tiles_per_seq=seqlen // tm),
        grid=(t // tm, nj),
        in_specs=[
            pl.BlockSpec((tm, D_MODEL), lambda i, j: (i, 0)),
            pl.BlockSpec((halo, D_MODEL), lambda i, j: (jnp.maximum(i * halo_blocks - 1, 0), 0)),
            pl.BlockSpec((1, D_MODEL), lambda i, j: (0, 0)),
            pl.BlockSpec((D_MODEL, tn), lambda i, j: (0, j)),
            pl.BlockSpec((D_MODEL, tn), lambda i, j: (0, nj + j)),
            pl.BlockSpec((CONV_FFN, tn), lambda i, j: (0, j)),
            pl.BlockSpec((CONV_FFN, tn), lambda i, j: (0, nj + j)),
            pl.BlockSpec((1, tn), lambda i, j: (0, j)),
            pl.BlockSpec((1, tn), lambda i, j: (0, nj + j)),
            pl.BlockSpec((tn, D_MODEL), lambda i, j: (j, 0)),
            pl.BlockSpec((1, D_MODEL), lambda i, j: (0, 0)),
        ],
        out_specs=pl.BlockSpec((tm, D_MODEL), lambda i, j: (i, 0)),
        out_shape=jax.ShapeDtypeStruct((t, D_MODEL), F32),
        scratch_shapes=[
            pltpu.VMEM((tm + halo, D_MODEL), BF16),
            pltpu.VMEM((tm + halo, tn), F32),
            pltpu.VMEM((tm + halo, tn), F32),
            pltpu.VMEM((tm, D_MODEL), F32),
        ],
        compiler_params=_cparams(2),
        name="ffn",
    )(h1, h1, norm_w, w_up, w_up, conv_w, conv_w, conv_b, conv_b, w_down, final_w)


def _mamba_params(dt_bias, a_log, d_a, norm_w, w_proj):
    def grouped(v, width):
        lead = v.shape[:-1]
        v = v.reshape(lead + (N_GROUPS, width))
        return jnp.moveaxis(v, -2, 0)

    def head_lanes(v):
        v = v.reshape(N_GROUPS, HEADS_PER_GROUP)
        v = jnp.pad(v, ((0, 0), (0, DT_LANE_STRIDE - HEADS_PER_GROUP)))
        return v.reshape(1, LANES)

    lane_head = jnp.arange(LANES)
    ch_head = jnp.arange(GROUP_CH) // HEAD_DIM
    e = []
    for g in range(N_GROUPS):
        e.append((lane_head[:, None] == (DT_LANE_STRIDE * g + ch_head)[None, :]))
    e = jnp.stack(e).astype(BF16)
    return {
        "dtb": head_lanes(dt_bias),
        "aneg": head_lanes(-jnp.exp(a_log)),
        "dexp": grouped(jnp.repeat(d_a, HEAD_DIM)[None, :], GROUP_CH),
        "e": e,
        "nw": grouped(norm_w[None, :], GROUP_CH),
        "wproj": w_proj.astype(BF16).reshape(N_GROUPS // 2, 2 * GROUP_CH, D_MODEL),
    }


def _s5_params(lam_re, lam_im, log_dt, b_re, b_im, c_re, c_im, d_skip, w_glu):
    lre = jnp.minimum(lam_re, EIG_MAX)
    lim = lam_im
    dt = jnp.exp(log_dt)[:, None]
    mag = jnp.exp(lre * dt)
    lbr = mag * jnp.cos(lim * dt)
    lbi = mag * jnp.sin(lim * dt)
    nr, ni = lbr - 1.0, lbi
    den = lre * lre + lim * lim
    cr = (nr * lre + ni * lim) / den
    ci = (ni * lre - nr * lim) / den
    bbr = cr[..., None] * b_re - ci[..., None] * b_im
    bbi = cr[..., None] * b_im + ci[..., None] * b_re
    gl = S5_NG // S5_BLOCKS
    eye = jnp.eye(gl, dtype=F32)

    def in_map(v):
        v = v.reshape(S5_BLOCKS, gl, S5_STATE, S5_GROUP).transpose(0, 1, 3, 2)
        v = jnp.einsum("sgcp,gh->sgchp", v, eye)
        return v.reshape(S5_BLOCKS, gl * S5_GROUP, gl * S5_STATE)

    def out_map(v):
        v = v.reshape(S5_BLOCKS, gl, S5_GROUP, S5_STATE).transpose(0, 1, 3, 2)
        v = jnp.einsum("sgpc,gh->sgphc", v, eye)
        return v.reshape(S5_BLOCKS, gl * S5_STATE, gl * S5_GROUP)

    def scan_coef(v):
        return v.reshape(S5_BLOCKS, S5_HALF, LANES).transpose(1, 0, 2)

    return {
        "wb": jnp.concatenate([in_map(bbr), in_map(bbi)], axis=2).astype(BF16),
        "wc": jnp.concatenate([out_map(c_re), -out_map(c_im)], axis=1).astype(BF16),
        "lr": scan_coef(lbr),
        "li": scan_coef(lbi),
        "d": d_skip[None, :],
        "wglu": w_glu,
    }


def _dt_weights(w_t):
    w_dt = w_t[N_DIRECT:N_DIRECT + N_HEADS].T.reshape(D_MODEL, N_GROUPS, HEADS_PER_GROUP)
    w_dt = jnp.pad(w_dt, ((0, 0), (0, 0), (0, DT_LANE_STRIDE - HEADS_PER_GROUP)))
    return w_dt.reshape(D_MODEL, LANES).astype(BF16)


def kernel(x, norm_mix_w, w_in, conv_a_w, conv_a_b, dt_bias, a_log, d_a, norm_a_w, w_proj_a,
           s5_lam_re, s5_lam_im, s5_log_dt, s5_b_re, s5_b_im, s5_c_re, s5_c_im, s5_d, w_s5_glu,
           w_out, norm_ffn_w, w_up, conv_ffn_w, conv_ffn_b, w_down, norm_final_w):
    batch, seqlen, _ = x.shape
    assert w_in.shape[0] == 1, "single-layer block: the final norm is fused into the FFN stage"
    h = x.reshape(batch * seqlen, D_MODEL)
    w_t = jnp.swapaxes(w_in[0], 0, 1)
    hn, dt_raw = _norm_dt(h, norm_mix_w[0][None, :], _dt_weights(w_t))
    proj = _inproj(hn, w_t, conv_a_w[0], conv_a_b[0][None, :], seqlen)
    mp = _mamba_params(dt_bias[0], a_log[0], d_a[0], norm_a_w[0], w_proj_a[0])
    ya_gated = _mamba(proj, dt_raw, mp, batch, seqlen)
    sp = _s5_params(s5_lam_re[0], s5_lam_im[0], s5_log_dt[0], s5_b_re[0], s5_b_im[0],
                    s5_c_re[0], s5_c_im[0], s5_d[0], w_s5_glu[0])
    merged = _s5(proj, ya_gated, sp, batch, seqlen)
    h1 = _merge_proj(h, merged, w_out[0])
    out = _ffn(h1, norm_ffn_w[0][None, :], w_up[0].astype(BF16), conv_ffn_w[0],
               conv_ffn_b[0][None, :], w_down[0].astype(BF16), norm_final_w[None, :], seqlen)
    return out.reshape(batch, seqlen, D_MODEL)
```

```python
import functools
import math

import jax
import jax.numpy as jnp
from jax import lax
from jax.experimental import pallas as pl
from jax.experimental.pallas import tpu as pltpu

F32 = jnp.float32
BF16 = jnp.bfloat16

D_MODEL = 2048
D_INNER = 3072
HEAD_DIM = 64
N_HEADS = 48
N_GROUPS = 8
HEADS_PER_GROUP = 6
GROUP_CH = D_INNER // N_GROUPS
D_STATE = 128
CONV_A = 4
CHUNK = 256
D_S5 = 1024
S5_GROUP = 16
S5_NG = 64
S5_STATE = 64
D_FF = 5632
CONV_FFN = 3
EPS = 1e-6
EIG_MAX = -1e-4

LANES = 128
SUBLANES = 8
BF16_ROWS = 16
VMEM_LIMIT = 56 * 1024 * 1024

COL_Z = 0
COL_XS = 3072
COL_B = 6144
COL_C = 7168
COL_GA = 8192
COL_GB = 10240
COL_U = 12288
N_MAIN = 13312
N_XBC = D_INNER + 2 * N_GROUPS * D_STATE
N_DIRECT = D_INNER + N_XBC
DT_LANE_STRIDE = 16
NEG_LOG2E = -1.4426950408889634


def _sigmoid(v):
    return 1.0 / (1.0 + jnp.exp2(v * NEG_LOG2E))


def _silu(v):
    return v * _sigmoid(v)


def _softplus(v):
    return jnp.maximum(v, 0.0) + jnp.log1p(jnp.exp(-jnp.abs(v)))


def _gelu_tanh(v):
    c = math.sqrt(2.0 / math.pi)
    return 0.5 * v * (1.0 + jnp.tanh(c * (v + 0.044715 * (v * v * v))))


def _split3(v):
    hi = v.astype(BF16)
    r1 = v - hi.astype(F32)
    mid = r1.astype(BF16)
    lo = (r1 - mid.astype(F32)).astype(BF16)
    return hi, mid, lo


def _dot(a, b):
    return jnp.dot(a, b, preferred_element_type=F32)


def _dot3(v, m_bf16):
    hi, mid, lo = _split3(v)
    return _dot(hi, m_bf16) + _dot(mid, m_bf16) + _dot(lo, m_bf16)


def _cparams(n_axes):
    return pltpu.CompilerParams(dimension_semantics=("arbitrary",) * n_axes,
                                vmem_limit_bytes=VMEM_LIMIT)


def _norm_body(x_ref, nw_ref, wdt_ref, hn_ref, dt_ref):
    x = x_ref[...]
    ms = jnp.mean(x * x, axis=-1, keepdims=True)
    hn = ((x * lax.rsqrt(ms + EPS)) * nw_ref[...]).astype(BF16)
    hn_ref[...] = hn
    dt_ref[...] = _dot(hn, wdt_ref[...])


def _norm_dt(x2, norm_w, w_dt, tm=512):
    t = x2.shape[0]
    return pl.pallas_call(
        _norm_body,
        grid=(t // tm,),
        in_specs=[
            pl.BlockSpec((tm, D_MODEL), lambda i: (i, 0)),
            pl.BlockSpec((1, D_MODEL), lambda i: (0, 0)),
            pl.BlockSpec((D_MODEL, LANES), lambda i: (0, 0)),
        ],
        out_specs=[
            pl.BlockSpec((tm, D_MODEL), lambda i: (i, 0)),
            pl.BlockSpec((tm, LANES), lambda i: (i, 0)),
        ],
        out_shape=[
            jax.ShapeDtypeStruct((t, D_MODEL), BF16),
            jax.ShapeDtypeStruct((t, LANES), F32),
        ],
        compiler_params=_cparams(1),
        name="norm_dt",
    )(x2, norm_w, w_dt)


IN_TILE = 1024
IN_SUB = 256
IN_ROWS = 1024
J_XS = COL_XS // IN_TILE
J_GA = COL_GA // IN_TILE
J_U = COL_U // IN_TILE
ROW_U = N_DIRECT + N_HEADS
ROW_GA = ROW_U + D_S5


def _inproj_row(j):
    assert ROW_GA % SUBLANES == 0 and ROW_U % SUBLANES == 0
    tile8 = IN_TILE // SUBLANES
    row8 = jnp.where(j < J_GA, j * tile8,
                     jnp.where(j < J_U, ROW_GA // SUBLANES + (j - J_GA) * tile8, ROW_U // SUBLANES))
    return row8 * SUBLANES


CAST_STEPS = 32


def _inproj_body(hn_ref, wt_ref, cw_ref, cb_ref, *rest, tm, tiles_per_seq, n_cast):
    cast_in = rest[:n_cast]
    o_ref = rest[n_cast]
    cast_out = rest[n_cast + 1:2 * n_cast + 1]
    wbf, xp, carry = rest[2 * n_cast + 1:]
    j = pl.program_id(0)
    i = pl.program_id(1)
    halo = SUBLANES

    @pl.when(j * pl.num_programs(1) + i < CAST_STEPS)
    def _():
        for src, dst in zip(cast_in, cast_out):
            dst[...] = src[...].astype(BF16)

    @pl.when(i == 0)
    def _():
        for n in range(IN_TILE // IN_SUB):
            sl = slice(n * IN_SUB, (n + 1) * IN_SUB)
            wbf[:, sl] = wt_ref[sl, :].T.astype(BF16)

    def tiles(epilogue):
        for c0 in range(0, IN_TILE, 2 * IN_SUB):
            for r0 in range(0, tm, IN_ROWS):
                hn = hn_ref[r0:r0 + IN_ROWS, :]
                for sl in (slice(c0, c0 + IN_SUB), slice(c0 + IN_SUB, c0 + 2 * IN_SUB)):
                    r = _dot(hn, wbf[:, sl])
                    o_ref[r0:r0 + IN_ROWS, sl] = epilogue(r, r0, sl).astype(o_ref.dtype)

    @pl.when(j < J_XS)
    def _():
        tiles(lambda r, r0, sl: _silu(r))

    @pl.when((j >= J_XS) & (j < J_GA))
    def _():
        first = (i % tiles_per_seq) == 0

        def conv_silu(r, r0, sl):
            if r0 == 0:
                before = carry[:, sl]
                xp[0:halo, sl] = jnp.where(first, jnp.zeros_like(before), before)
            top = halo + r0
            xp[top:top + IN_ROWS, sl] = r
            if r0 + IN_ROWS == tm:
                carry[:, sl] = r[IN_ROWS - halo:IN_ROWS, :]
            acc = cb_ref[:, sl] + cw_ref[3:4, sl] * r
            acc = acc + cw_ref[2:3, sl] * xp[pl.ds(top - 1, IN_ROWS), sl]
            acc = acc + cw_ref[1:2, sl] * xp[pl.ds(top - 2, IN_ROWS), sl]
            acc = acc + cw_ref[0:1, sl] * xp[pl.ds(top - 3, IN_ROWS), sl]
            return _silu(acc)

        tiles(conv_silu)

    @pl.when((j >= J_GA) & (j < J_U))
    def _():
        tiles(lambda r, r0, sl: _sigmoid(r))

    @pl.when(j == J_U)
    def _():
        tiles(lambda r, r0, sl: r)


def _inproj(hn, w_t, conv_w, conv_b, later_weights, seqlen, tm=IN_ROWS):
    t = hn.shape[0]
    n_conv = N_XBC // IN_TILE
    row_tiles = t // tm
    assert (N_MAIN // IN_TILE) * row_tiles >= CAST_STEPS

    def slab(w):
        rows = w.shape[0] // CAST_STEPS
        assert rows * CAST_STEPS == w.shape[0] and rows % BF16_ROWS == 0
        return pl.BlockSpec((rows, w.shape[1]),
                            lambda j, i: (jnp.minimum(j * row_tiles + i, CAST_STEPS - 1), 0))

    slabs = [slab(w) for w in later_weights]
    outs = pl.pallas_call(
        functools.partial(_inproj_body, tm=tm, tiles_per_seq=seqlen // tm,
                          n_cast=len(later_weights)),
        grid=(N_MAIN // IN_TILE, row_tiles),
        in_specs=[
            pl.BlockSpec((tm, D_MODEL), lambda j, i: (i, 0)),
            pl.BlockSpec((pl.Element(IN_TILE), pl.Element(D_MODEL)),
                         lambda j, i: (_inproj_row(j), 0)),
            pl.BlockSpec((CONV_A, IN_TILE), lambda j, i: (0, jnp.clip(j - J_XS, 0, n_conv - 1))),
            pl.BlockSpec((1, IN_TILE), lambda j, i: (0, jnp.clip(j - J_XS, 0, n_conv - 1))),
        ] + slabs,
        out_specs=[pl.BlockSpec((tm, IN_TILE), lambda j, i: (i, j))] + slabs,
        out_shape=[jax.ShapeDtypeStruct((t, N_MAIN), BF16)]
        + [jax.ShapeDtypeStruct(w.shape, BF16) for w in later_weights],
        scratch_shapes=[
            pltpu.VMEM((D_MODEL, IN_TILE), BF16),
            pltpu.VMEM((tm + SUBLANES, IN_TILE), F32),
            pltpu.VMEM((SUBLANES, IN_TILE), F32),
        ],
        compiler_params=_cparams(2),
        name="inproj",
    )(hn, w_t, conv_w, conv_b, *later_weights)
    return outs[0], outs[1:]


def _mamba_body(sz_ref, xs_ref, b_ref, c_ref, sga_ref, dtraw_ref,
                dtb_ref, aneg_ref, dexp_ref, e_ref, nw_ref, wproj_ref,
                o_ref,
                st_scr, acc_scr):
    q = CHUNK

    @pl.when(pl.program_id(1) == 0)
    def _():
        st_scr[...] = jnp.zeros_like(st_scr)

    dt = _softplus(dtraw_ref[...] + dtb_ref[...])
    a = aneg_ref[...] * dt
    row = lax.broadcasted_iota(jnp.int32, (q, q), 0)
    col = lax.broadcasted_iota(jnp.int32, (q, q), 1)
    causal = row >= col
    tril = jnp.where(causal, 1.0, 0.0).astype(BF16)
    ah, am, al = _split3(a)
    cs = _dot(tril, ah) + _dot(tril, am) + _dot(tril, al)
    ecs = jnp.exp(cs)
    dt_bf = dt.astype(BF16)
    ecs_bf = ecs.astype(BF16)
    dst_bf = jnp.exp(cs[q - 1:q, :] - cs).astype(BF16)
    ecs_last = jnp.broadcast_to(ecs[q - 1:q, :], (SUBLANES, LANES))

    lane = lax.broadcasted_iota(jnp.int32, (q, LANES), 1)
    first_half = lane < HEAD_DIM

    hq = q // 2
    causal_top = causal[0:hq, 0:hq]
    causal_bot = causal[hq:q, :]
    nt = (((1,), (1,)), ((), ()))
    tn = (((0,), (0,)), ((), ()))

    def one_group(g):
        ch = slice(g * GROUP_CH, (g + 1) * GROUP_CH)
        st = slice(g * D_STATE, (g + 1) * D_STATE)
        xs_g = xs_ref[:, ch].astype(F32)
        b_bf = b_ref[:, st]
        c_bf = c_ref[:, st]
        e_g = e_ref[g]
        dtx = _dot(dt_bf, e_g)
        ecsx = _dot(ecs_bf, e_g)
        dstx = _dot(dst_bf, e_g)
        sdec = _dot3(ecs_last, e_g)[0:1, :]
        xdt = xs_g * dtx
        x_bf = xdt.astype(BF16)
        xd_bf = (xdt * dstx).astype(BF16)
        cb_top = lax.dot_general(c_bf[0:hq], b_bf[0:hq], nt, preferred_element_type=F32)
        cb_bot = lax.dot_general(c_bf[hq:q], b_bf, nt, preferred_element_type=F32)
        cs_cols = cs if g == 0 else pltpu.roll(cs, LANES - DT_LANE_STRIDE * g, axis=1)
        cs_rows = cs_cols.T
        pieces = []
        for p in range(HEADS_PER_GROUP // 2):
            xp_ = x_bf[:, p * LANES:(p + 1) * LANES]
            acc = None
            for k in range(2):
                r = 2 * p + k
                d_top = cs_cols[0:hq, r:r + 1] - cs_rows[r:r + 1, 0:hq]
                d_bot = cs_cols[hq:q, r:r + 1] - cs_rows[r:r + 1, :]
                m_top = (jnp.exp(jnp.where(causal_top, d_top, -jnp.inf)) * cb_top).astype(BF16)
                m_bot = (jnp.exp(jnp.where(causal_bot, d_bot, -jnp.inf)) * cb_bot).astype(BF16)
                keep = first_half if k == 0 else jnp.logical_not(first_half)
                xk = jnp.where(keep, xp_, jnp.zeros_like(xp_))
                t = jnp.concatenate([_dot(m_top, xk[0:hq]), _dot(m_bot, xk)], axis=0)
                acc = t if acc is None else acc + t
            pieces.append(acc)
        y = jnp.concatenate(pieces, axis=1)
        s_old = st_scr[g]
        y = y + _dot(c_bf, s_old.astype(BF16)) * ecsx
        st_scr[g] = sdec * s_old + lax.dot_general(b_bf, xd_bf, tn, preferred_element_type=F32)
        y = y + dexp_ref[g] * xs_g
        y = y * sz_ref[:, ch].astype(F32)
        ms = jnp.mean(y * y, axis=-1, keepdims=True)
        return ((y * lax.rsqrt(ms + EPS)) * nw_ref[g]).astype(BF16)

    for p in range(N_GROUPS // 2):
        yn = jnp.concatenate([one_group(2 * p), one_group(2 * p + 1)], axis=1)
        part = _dot(yn, wproj_ref[p])
        if p == 0:
            acc_scr[...] = part
        else:
            acc_scr[...] += part
    o_ref[...] = (acc_scr[...] * sga_ref[...].astype(F32)).astype(o_ref.dtype)


def _mamba(proj, dt_raw, p, batch, seqlen):
    q = CHUNK
    nc = seqlen // q
    t = batch * seqlen

    def rows(b, c):
        return b * nc + c

    def colblk(width, off):
        return pl.BlockSpec((q, width), lambda b, c: (rows(b, c), off // width))

    def full(shape):
        return pl.BlockSpec(shape, lambda b, c: (0,) * len(shape))

    return pl.pallas_call(
        _mamba_body,
        grid=(batch, nc),
        in_specs=[
            colblk(D_INNER, COL_Z), colblk(D_INNER, COL_XS),
            colblk(N_GROUPS * D_STATE, COL_B), colblk(N_GROUPS * D_STATE, COL_C),
            colblk(D_MODEL, COL_GA),
            pl.BlockSpec((q, LANES), lambda b, c: (rows(b, c), 0)),
            full((1, LANES)), full((1, LANES)),
            full((N_GROUPS, 1, GROUP_CH)), full((N_GROUPS, LANES, GROUP_CH)),
            full((N_GROUPS, 1, GROUP_CH)), full((N_GROUPS // 2, 2 * GROUP_CH, D_MODEL)),
        ],
        out_specs=pl.BlockSpec((q, D_MODEL), lambda b, c: (rows(b, c), 0)),
        out_shape=jax.ShapeDtypeStruct((t, D_MODEL), BF16),
        scratch_shapes=[
            pltpu.VMEM((N_GROUPS, D_STATE, GROUP_CH), F32),
            pltpu.VMEM((q, D_MODEL), F32),
        ],
        compiler_params=_cparams(2),
        name="mamba",
    )(proj, proj, proj, proj, proj, dt_raw,
      p["dtb"], p["aneg"], p["dexp"], p["e"], p["nw"], p["wproj"])


S5_BLOCKS = D_S5 // LANES
S5_SLABS = 2 * (S5_NG // S5_BLOCKS) * S5_STATE // LANES
S5_HALF = S5_SLABS // 2
S5_PITCH = 12
S5_UNROLL = 8


def _s5_body(u_ref, gb_ref, ya_ref, wb_ref, wc_ref, lr_ref, li_ref, d_ref, wglu_ref,
             o_ref, xs_scr, st_scr, y_scr, wglu_bf, *, tc):
    @pl.when((pl.program_id(0) == 0) & (pl.program_id(1) == 0))
    def _():
        wglu_bf[...] = wglu_ref[...].astype(BF16)

    @pl.when(pl.program_id(1) == 0)
    def _():
        st_scr[...] = jnp.zeros_like(st_scr)

    u = u_ref[...]
    for s in range(S5_BLOCKS):
        bu = _dot(u[:, s * LANES:(s + 1) * LANES], wb_ref[s])
        for k in range(S5_SLABS):
            xs_scr[k, pl.ds(s, tc, stride=S5_PITCH), :] = bu[:, k * LANES:(k + 1) * LANES]

    lr = [lr_ref[k] for k in range(S5_HALF)]
    li = [li_ref[k] for k in range(S5_HALF)]

    def steps(it, st):
        base = pl.multiple_of(it * (S5_UNROLL * S5_PITCH), SUBLANES)
        for n in range(S5_UNROLL):
            r0 = base + n * S5_PITCH
            new_re, new_im = [], []
            for k in range(S5_HALF):
                re, im = st[k], st[S5_HALF + k]
                nre = (lr[k] * re - li[k] * im) + xs_scr[k, pl.ds(r0, SUBLANES), :]
                nim = (lr[k] * im + li[k] * re) + xs_scr[S5_HALF + k, pl.ds(r0, SUBLANES), :]
                xs_scr[k, pl.ds(r0, SUBLANES), :] = nre
                xs_scr[S5_HALF + k, pl.ds(r0, SUBLANES), :] = nim
                new_re.append(nre)
                new_im.append(nim)
            st = tuple(new_re + new_im)
        return st

    st0 = tuple(st_scr[k] for k in range(S5_SLABS))
    st = lax.fori_loop(0, tc // S5_UNROLL, steps, st0)
    for k in range(S5_SLABS):
        st_scr[k] = st[k]

    for s in range(S5_BLOCKS):
        xb = jnp.concatenate(
            [xs_scr[k, pl.ds(s, tc, stride=S5_PITCH), :] for k in range(S5_SLABS)], axis=1)
        y_scr[:, s * LANES:(s + 1) * LANES] = _dot(xb.astype(BF16), wc_ref[s])

    y = y_scr[...] + d_ref[...] * u.astype(F32)
    y = _gelu_tanh(y).astype(BF16)
    vg = _dot(y, wglu_bf[...])
    yb = vg[:, :D_MODEL] * _sigmoid(vg[:, D_MODEL:])
    merged = ya_ref[...].astype(F32) + gb_ref[...].astype(F32) * yb
    o_ref[...] = merged.astype(o_ref.dtype)


def _s5(proj, ya_gated, p, batch, seqlen, tc=256):
    nc = seqlen // tc
    t = batch * seqlen

    def rows(b, c):
        return b * nc + c

    def full(shape):
        return pl.BlockSpec(shape, lambda b, c: (0,) * len(shape))

    return pl.pallas_call(
        functools.partial(_s5_body, tc=tc),
        grid=(batch, nc),
        in_specs=[
            pl.BlockSpec((tc, D_S5), lambda b, c: (rows(b, c), COL_U // D_S5)),
            pl.BlockSpec((tc, D_MODEL), lambda b, c: (rows(b, c), COL_GB // D_MODEL)),
            pl.BlockSpec((tc, D_MODEL), lambda b, c: (rows(b, c), 0)),
            full((S5_BLOCKS, LANES, S5_SLABS * LANES)),
            full((S5_BLOCKS, S5_SLABS * LANES, LANES)),
            full((S5_HALF, SUBLANES, LANES)), full((S5_HALF, SUBLANES, LANES)),
            full((1, D_S5)),
            pl.BlockSpec((D_S5, 2 * D_MODEL), lambda b, c: (0, 0), pipeline_mode=pl.Buffered(1)),
        ],
        out_specs=pl.BlockSpec((tc, D_MODEL), lambda b, c: (rows(b, c), 0)),
        out_shape=jax.ShapeDtypeStruct((t, D_MODEL), BF16),
        scratch_shapes=[
            pltpu.VMEM((S5_SLABS, tc * S5_PITCH, LANES), F32),
            pltpu.VMEM((S5_SLABS, SUBLANES, LANES), F32),
            pltpu.VMEM((tc, D_S5), F32),
            pltpu.VMEM((D_S5, 2 * D_MODEL), BF16),
        ],
        compiler_params=_cparams(2),
        name="s5",
    )(proj, proj, ya_gated, p["wb"], p["wc"], p["lr"], p["li"], p["d"], p["wglu"])


def _merge_body(x_ref, m_ref, w_ref, o_ref, wbf):
    @pl.when(pl.program_id(0) == 0)
    def _():
        wbf[...] = w_ref[...].astype(BF16)

    o_ref[...] = x_ref[...] + _dot(m_ref[...], wbf[...])


def _merge_proj(x2, merged, w_out, tm=512):
    t = x2.shape[0]
    return pl.pallas_call(
        _merge_body,
        grid=(t // tm,),
        in_specs=[
            pl.BlockSpec((tm, D_MODEL), lambda i: (i, 0)),
            pl.BlockSpec((tm, D_MODEL), lambda i: (i, 0)),
            pl.BlockSpec((D_MODEL, D_MODEL), lambda i: (0, 0), pipeline_mode=pl.Buffered(1)),
        ],
        out_specs=pl.BlockSpec((tm, D_MODEL), lambda i: (i, 0)),
        out_shape=jax.ShapeDtypeStruct((t, D_MODEL), F32),
        scratch_shapes=[pltpu.VMEM((D_MODEL, D_MODEL), BF16)],
        compiler_params=_cparams(1),
        name="merge_proj",
    )(x2, merged, w_out)


def _ffn_body(h_ref, halo_ref, nw_ref, wg_ref, wv_ref, cwg_ref, cwv_ref, cbg_ref, cbv_ref,
              wd_ref, fw_ref, o_ref, lhs_scr, upg_scr, upv_scr, acc_scr, *, tm, tiles_per_seq):
    i = pl.program_id(0)
    j = pl.program_id(1)
    halo = BF16_ROWS

    def norm(v):
        ms = jnp.mean(v * v, axis=-1, keepdims=True)
        return ((v * lax.rsqrt(ms + EPS)) * nw_ref[...]).astype(BF16)

    @pl.when(j == 0)
    def _():
        lhs_scr[halo:halo + tm, :] = norm(h_ref[...])
        first = (i % tiles_per_seq) == 0
        hn_halo = norm(halo_ref[...])
        lhs_scr[0:halo, :] = jnp.where(first, jnp.zeros_like(hn_halo), hn_halo)
        acc_scr[...] = jnp.zeros_like(acc_scr)

    lhs = lhs_scr[...]
    upg_scr[...] = _dot(lhs, wg_ref[...])
    upv_scr[...] = _dot(lhs, wv_ref[...])

    def conv(up_scr, cw_ref, cb_ref):
        acc = cb_ref[...] + cw_ref[2:3, :] * up_scr[pl.ds(halo, tm), :]
        acc = acc + cw_ref[1:2, :] * up_scr[pl.ds(halo - 1, tm), :]
        acc = acc + cw_ref[0:1, :] * up_scr[pl.ds(halo - 2, tm), :]
        return acc

    act = (_silu(conv(upg_scr, cwg_ref, cbg_ref)) * conv(upv_scr, cwv_ref, cbv_ref)).astype(BF16)
    acc_scr[...] += _dot(act, wd_ref[...])

    @pl.when(j == pl.num_programs(1) - 1)
    def _():
        h2 = h_ref[...] + acc_scr[...]
        ms = jnp.mean(h2 * h2, axis=-1, keepdims=True)
        o_ref[...] = (h2 * lax.rsqrt(ms + EPS)) * fw_ref[...]


def _ffn(h1, norm_w, w_up, conv_w, conv_b, w_down, final_w, seqlen, tm=512, tn=512):
    t = h1.shape[0]
    nj = D_FF // tn
    halo = BF16_ROWS
    halo_blocks = tm // halo
    return pl.pallas_call(
        functools.partial(_ffn_body, tm=tm, tiles_per_seq=seqlen // tm),
        grid=(t // tm, nj),
        in_specs=[
            pl.BlockSpec((tm, D_MODEL), lambda i, j: (i, 0)),
            pl.BlockSpec((halo, D_MODEL), lambda i, j: (jnp.maximum(i * halo_blocks - 1, 0), 0)),
            pl.BlockSpec((1, D_MODEL), lambda i, j: (0, 0)),
            pl.BlockSpec((D_MODEL, tn), lambda i, j: (0, j)),
            pl.BlockSpec((D_MODEL, tn), lambda i, j: (0, nj + j)),
            pl.BlockSpec((CONV_FFN, tn), lambda i, j: (0, j)),
            pl.BlockSpec((CONV_FFN, tn), lambda i, j: (0, nj + j)),
            pl.BlockSpec((1, tn), lambda i, j: (0, j)),
            pl.BlockSpec((1, tn), lambda i, j: (0, nj + j)),
            pl.BlockSpec((tn, D_MODEL), lambda i, j: (j, 0)),
            pl.BlockSpec((1, D_MODEL), lambda i, j: (0, 0)),
        ],
        out_specs=pl.BlockSpec((tm, D_MODEL), lambda i, j: (i, 0)),
        out_shape=jax.ShapeDtypeStruct((t, D_MODEL), F32),
        scratch_shapes=[
            pltpu.VMEM((tm + halo, D_MODEL), BF16),
            pltpu.VMEM((tm + halo, tn), F32),
            pltpu.VMEM((tm + halo, tn), F32),
            pltpu.VMEM((tm, D_MODEL), F32),
        ],
        compiler_params=_cparams(2),
        name="ffn",
    )(h1, h1, norm_w, w_up, w_up, conv_w, conv_w, conv_b, conv_b, w_down, final_w)


def _mamba_params(dt_bias, a_log, d_a, norm_w, w_proj):
    def grouped(v, width):
        lead = v.shape[:-1]
        v = v.reshape(lead + (N_GROUPS, width))
        return jnp.moveaxis(v, -2, 0)

    def head_lanes(v):
        v = v.reshape(N_GROUPS, HEADS_PER_GROUP)
        v = jnp.pad(v, ((0, 0), (0, DT_LANE_STRIDE - HEADS_PER_GROUP)))
        return v.reshape(1, LANES)

    lane_head = jnp.arange(LANES)
    ch_head = jnp.arange(GROUP_CH) // HEAD_DIM
    e = []
    for g in range(N_GROUPS):
        e.append((lane_head[:, None] == (DT_LANE_STRIDE * g + ch_head)[None, :]))
    e = jnp.stack(e).astype(BF16)
    return {
        "dtb": head_lanes(dt_bias),
        "aneg": head_lanes(-jnp.exp(a_log)),
        "dexp": grouped(jnp.repeat(d_a, HEAD_DIM)[None, :], GROUP_CH),
        "e": e,
        "nw": grouped(norm_w[None, :], GROUP_CH),
        "wproj": w_proj.reshape(N_GROUPS // 2, 2 * GROUP_CH, D_MODEL),
    }


def _s5_params(lam_re, lam_im, log_dt, b_re, b_im, c_re, c_im, d_skip, w_glu):
    lre = jnp.minimum(lam_re, EIG_MAX)
    lim = lam_im
    dt = jnp.exp(log_dt)[:, None]
    mag = jnp.exp(lre * dt)
    lbr = mag * jnp.cos(lim * dt)
    lbi = mag * jnp.sin(lim * dt)
    nr, ni = lbr - 1.0, lbi
    den = lre * lre + lim * lim
    cr = (nr * lre + ni * lim) / den
    ci = (ni * lre - nr * lim) / den
    bbr = cr[..., None] * b_re - ci[..., None] * b_im
    bbi = cr[..., None] * b_im + ci[..., None] * b_re
    gl = S5_NG // S5_BLOCKS
    eye = jnp.eye(gl, dtype=F32)

    def in_map(v):
        v = v.reshape(S5_BLOCKS, gl, S5_STATE, S5_GROUP).transpose(0, 1, 3, 2)
        v = jnp.einsum("sgcp,gh->sgchp", v, eye)
        return v.reshape(S5_BLOCKS, gl * S5_GROUP, gl * S5_STATE)

    def out_map(v):
        v = v.reshape(S5_BLOCKS, gl, S5_GROUP, S5_STATE).transpose(0, 1, 3, 2)
        v = jnp.einsum("sgpc,gh->sgphc", v, eye)
        return v.reshape(S5_BLOCKS, gl * S5_STATE, gl * S5_GROUP)

    def scan_coef(v):
        return v.reshape(S5_BLOCKS, S5_HALF, LANES).transpose(1, 0, 2)

    return {
        "wb": jnp.concatenate([in_map(bbr), in_map(bbi)], axis=2).astype(BF16),
        "wc": jnp.concatenate([out_map(c_re), -out_map(c_im)], axis=1).astype(BF16),
        "lr": scan_coef(lbr),
        "li": scan_coef(lbi),
        "d": d_skip[None, :],
        "wglu": w_glu,
    }


def _dt_weights(w_t):
    w_dt = w_t[N_DIRECT:N_DIRECT + N_HEADS].T.reshape(D_MODEL, N_GROUPS, HEADS_PER_GROUP)
    w_dt = jnp.pad(w_dt, ((0, 0), (0, 0), (0, DT_LANE_STRIDE - HEADS_PER_GROUP)))
    return w_dt.reshape(D_MODEL, LANES).astype(BF16)


def kernel(x, norm_mix_w, w_in, conv_a_w, conv_a_b, dt_bias, a_log, d_a, norm_a_w, w_proj_a,
           s5_lam_re, s5_lam_im, s5_log_dt, s5_b_re, s5_b_im, s5_c_re, s5_c_im, s5_d, w_s5_glu,
           w_out, norm_ffn_w, w_up, conv_ffn_w, conv_ffn_b, w_down, norm_final_w):
    batch, seqlen, _ = x.shape
    assert w_in.shape[0] == 1, "single-layer block: the final norm is fused into the FFN stage"
    h = x.reshape(batch * seqlen, D_MODEL)
    w_t = jnp.swapaxes(w_in[0], 0, 1)
    hn, dt_raw = _norm_dt(h, norm_mix_w[0][None, :], _dt_weights(w_t))
    proj, (w_proj_bf, w_up_bf, w_down_bf) = _inproj(
        hn, w_t, conv_a_w[0], conv_a_b[0][None, :], (w_proj_a[0], w_up[0], w_down[0]), seqlen)
    mp = _mamba_params(dt_bias[0], a_log[0], d_a[0], norm_a_w[0], w_proj_bf)
    ya_gated = _mamba(proj, dt_raw, mp, batch, seqlen)
    sp = _s5_params(s5_lam_re[0], s5_lam_im[0], s5_log_dt[0], s5_b_re[0], s5_b_im[0],
                    s5_c_re[0], s5_c_im[0], s5_d[0], w_s5_glu[0])
    merged = _s5(proj, ya_gated, sp, batch, seqlen)
    h1 = _merge_proj(h, merged, w_out[0])
    out = _ffn(h1, norm_ffn_w[0][None, :], w_up_bf, conv_ffn_w[0],
               conv_ffn_b[0][None, :], w_down_bf, norm_final_w[None, :], seqlen)
    return out.reshape(batch, seqlen, D_MODEL)
```

```python
import functools
import math

import jax
import jax.numpy as jnp
from jax import lax
from jax.experimental import pallas as pl
from jax.experimental.pallas import tpu as pltpu

F32 = jnp.float32
BF16 = jnp.bfloat16

D_MODEL = 2048
D_INNER = 3072
HEAD_DIM = 64
N_HEADS = 48
N_GROUPS = 8
HEADS_PER_GROUP = 6
GROUP_CH = D_INNER // N_GROUPS
D_STATE = 128
CONV_A = 4
CHUNK = 256
D_S5 = 1024
S5_GROUP = 16
S5_NG = 64
S5_STATE = 64
D_FF = 5632
CONV_FFN = 3
EPS = 1e-6
EIG_MAX = -1e-4

LANES = 128
SUBLANES = 8
BF16_ROWS = 16
VMEM_LIMIT = 56 * 1024 * 1024

COL_Z = 0
COL_XS = 3072
COL_B = 6144
COL_C = 7168
COL_GA = 8192
COL_GB = 10240
COL_U = 12288
N_MAIN = 13312
N_XBC = D_INNER + 2 * N_GROUPS * D_STATE
N_DIRECT = D_INNER + N_XBC
DT_LANE_STRIDE = 16
NEG_LOG2E = -1.4426950408889634


def _sigmoid(v):
    return 1.0 / (1.0 + jnp.exp2(v * NEG_LOG2E))


def _silu(v):
    return v * _sigmoid(v)


def _softplus(v):
    return jnp.maximum(v, 0.0) + jnp.log1p(jnp.exp(-jnp.abs(v)))


def _gelu_tanh(v):
    c = math.sqrt(2.0 / math.pi)
    return 0.5 * v * (1.0 + jnp.tanh(c * (v + 0.044715 * (v * v * v))))


def _split3(v):
    hi = v.astype(BF16)
    r1 = v - hi.astype(F32)
    mid = r1.astype(BF16)
    lo = (r1 - mid.astype(F32)).astype(BF16)
    return hi, mid, lo


def _dot(a, b):
    return jnp.dot(a, b, preferred_element_type=F32)


def _dot3(v, m_bf16):
    hi, mid, lo = _split3(v)
    return _dot(hi, m_bf16) + _dot(mid, m_bf16) + _dot(lo, m_bf16)


def _cparams(n_axes):
    return pltpu.CompilerParams(dimension_semantics=("arbitrary",) * n_axes,
                                vmem_limit_bytes=VMEM_LIMIT)


def _norm_body(x_ref, nw_ref, wdt_ref, hn_ref, dt_ref):
    x = x_ref[...]
    ms = jnp.mean(x * x, axis=-1, keepdims=True)
    hn = ((x * lax.rsqrt(ms + EPS)) * nw_ref[...]).astype(BF16)
    hn_ref[...] = hn
    dt_ref[...] = _dot(hn, wdt_ref[...])


def _norm_dt(x2, norm_w, w_dt, tm=512):
    t = x2.shape[0]
    return pl.pallas_call(
        _norm_body,
        grid=(t // tm,),
        in_specs=[
            pl.BlockSpec((tm, D_MODEL), lambda i: (i, 0)),
            pl.BlockSpec((1, D_MODEL), lambda i: (0, 0)),
            pl.BlockSpec((D_MODEL, LANES), lambda i: (0, 0)),
        ],
        out_specs=[
            pl.BlockSpec((tm, D_MODEL), lambda i: (i, 0)),
            pl.BlockSpec((tm, LANES), lambda i: (i, 0)),
        ],
        out_shape=[
            jax.ShapeDtypeStruct((t, D_MODEL), BF16),
            jax.ShapeDtypeStruct((t, LANES), F32),
        ],
        compiler_params=_cparams(1),
        name="norm_dt",
    )(x2, norm_w, w_dt)


IN_TILE = 1024
IN_SUB = 256
J_XS = COL_XS // IN_TILE
J_GA = COL_GA // IN_TILE
J_U = COL_U // IN_TILE
ROW_U = N_DIRECT + N_HEADS
ROW_GA = ROW_U + D_S5


def _inproj_row(j):
    assert ROW_GA % SUBLANES == 0 and ROW_U % SUBLANES == 0
    tile8 = IN_TILE // SUBLANES
    row8 = jnp.where(j < J_GA, j * tile8,
                     jnp.where(j < J_U, ROW_GA // SUBLANES + (j - J_GA) * tile8, ROW_U // SUBLANES))
    return row8 * SUBLANES


def _inproj_body(hn_ref, wt_ref, cw_ref, cb_ref, *rest, tm, tiles_per_seq, n_ride_z, n_ride_gate):
    n_cast = n_ride_z + n_ride_gate
    cast_in = rest[:n_cast]
    o_ref = rest[n_cast]
    cast_out = rest[n_cast + 1:2 * n_cast + 1]
    wbf, xp, carry = rest[2 * n_cast + 1:]
    j = pl.program_id(0)
    i = pl.program_id(1)
    halo = SUBLANES

    def ride_along(lo, hi):
        for src, dst in zip(cast_in[lo:hi], cast_out[lo:hi]):
            dst[...] = src[...].astype(BF16)

    @pl.when(i == 0)
    def _():
        for n in range(IN_TILE // IN_SUB):
            sl = slice(n * IN_SUB, (n + 1) * IN_SUB)
            wbf[:, sl] = wt_ref[sl, :].T.astype(BF16)

    def tiles(epilogue):
        hn = hn_ref[...]
        for c0 in range(0, IN_TILE, IN_SUB):
            sl = slice(c0, c0 + IN_SUB)
            o_ref[:, sl] = epilogue(_dot(hn, wbf[:, sl]), sl).astype(o_ref.dtype)

    @pl.when(j < J_XS)
    def _():
        ride_along(0, n_ride_z)
        tiles(lambda r, sl: _silu(r))

    @pl.when((j >= J_XS) & (j < J_GA))
    def _():
        first = (i % tiles_per_seq) == 0

        def conv_silu(r, sl):
            before = carry[:, sl]
            xp[0:halo, sl] = jnp.where(first, jnp.zeros_like(before), before)
            xp[halo:halo + tm, sl] = r
            carry[:, sl] = r[tm - halo:tm, :]
            acc = cb_ref[:, sl] + cw_ref[3:4, sl] * r
            acc = acc + cw_ref[2:3, sl] * xp[pl.ds(halo - 1, tm), sl]
            acc = acc + cw_ref[1:2, sl] * xp[pl.ds(halo - 2, tm), sl]
            acc = acc + cw_ref[0:1, sl] * xp[pl.ds(halo - 3, tm), sl]
            return _silu(acc)

        tiles(conv_silu)

    @pl.when((j >= J_GA) & (j < J_U))
    def _():
        ride_along(n_ride_z, n_cast)
        tiles(lambda r, sl: _sigmoid(r))

    @pl.when(j == J_U)
    def _():
        tiles(lambda r, sl: r)


def _inproj(hn, w_t, conv_w, conv_b, ride_z, ride_gate, seqlen, tm=1024):
    t = hn.shape[0]
    n_conv = N_XBC // IN_TILE
    row_tiles = t // tm

    def slab(w, first_step, n_steps):
        rows = w.shape[0] // n_steps
        assert rows * n_steps == w.shape[0] and rows % BF16_ROWS == 0
        return pl.BlockSpec(
            (rows, w.shape[1]),
            lambda j, i: (jnp.clip(j * row_tiles + i - first_step, 0, n_steps - 1), 0))

    later_weights = tuple(ride_z) + tuple(ride_gate)
    slabs = ([slab(w, 0, J_XS * row_tiles) for w in ride_z]
             + [slab(w, J_GA * row_tiles, (J_U - J_GA) * row_tiles) for w in ride_gate])
    outs = pl.pallas_call(
        functools.partial(_inproj_body, tm=tm, tiles_per_seq=seqlen // tm,
                          n_ride_z=len(ride_z), n_ride_gate=len(ride_gate)),
        grid=(N_MAIN // IN_TILE, row_tiles),
        in_specs=[
            pl.BlockSpec((tm, D_MODEL), lambda j, i: (i, 0)),
            pl.BlockSpec((pl.Element(IN_TILE), pl.Element(D_MODEL)),
                         lambda j, i: (_inproj_row(j), 0)),
            pl.BlockSpec((CONV_A, IN_TILE), lambda j, i: (0, jnp.clip(j - J_XS, 0, n_conv - 1))),
            pl.BlockSpec((1, IN_TILE), lambda j, i: (0, jnp.clip(j - J_XS, 0, n_conv - 1))),
        ] + slabs,
        out_specs=[pl.BlockSpec((tm, IN_TILE), lambda j, i: (i, j))] + slabs,
        out_shape=[jax.ShapeDtypeStruct((t, N_MAIN), BF16)]
        + [jax.ShapeDtypeStruct(w.shape, BF16) for w in later_weights],
        scratch_shapes=[
            pltpu.VMEM((D_MODEL, IN_TILE), BF16),
            pltpu.VMEM((tm + SUBLANES, IN_TILE), F32),
            pltpu.VMEM((SUBLANES, IN_TILE), F32),
        ],
        compiler_params=_cparams(2),
        name="inproj",
    )(hn, w_t, conv_w, conv_b, *later_weights)
    return outs[0], outs[1:]


def _mamba_body(sz_ref, xs_ref, b_ref, c_ref, sga_ref, dtraw_ref,
                dtb_ref, aneg_ref, dexp_ref, e_ref, nw_ref, wproj_ref,
                o_ref,
                st_scr, acc_scr):
    q = CHUNK

    @pl.when(pl.program_id(1) == 0)
    def _():
        st_scr[...] = jnp.zeros_like(st_scr)

    dt = _softplus(dtraw_ref[...] + dtb_ref[...])
    a = aneg_ref[...] * dt
    row = lax.broadcasted_iota(jnp.int32, (q, q), 0)
    col = lax.broadcasted_iota(jnp.int32, (q, q), 1)
    causal = row >= col
    tril = jnp.where(causal, 1.0, 0.0).astype(BF16)
    ah, am, al = _split3(a)
    cs = _dot(tril, ah) + _dot(tril, am) + _dot(tril, al)
    ecs = jnp.exp(cs)
    ecs_bf = ecs.astype(BF16)
    wst_bf = (dt * jnp.exp(cs[q - 1:q, :] - cs)).astype(BF16)
    ecs_last = jnp.broadcast_to(ecs[q - 1:q, :], (SUBLANES, LANES))
    cs_src = cs - jnp.log(dt)

    lane = lax.broadcasted_iota(jnp.int32, (q, LANES), 1)
    first_half = lane < HEAD_DIM

    hq = q // 2
    causal_top = causal[0:hq, 0:hq]
    causal_bot = causal[hq:q, :]
    nt = (((1,), (1,)), ((), ()))
    tn = (((0,), (0,)), ((), ()))

    def one_group(g):
        ch = slice(g * GROUP_CH, (g + 1) * GROUP_CH)
        st = slice(g * D_STATE, (g + 1) * D_STATE)
        x_bf = xs_ref[:, ch]
        xs_g = x_bf.astype(F32)
        b_bf = b_ref[:, st]
        c_bf = c_ref[:, st]
        e_g = e_ref[g]
        ecsx = _dot(ecs_bf, e_g)
        wstx = _dot(wst_bf, e_g)
        sdec = _dot3(ecs_last, e_g)[0:1, :]
        xd_bf = (xs_g * wstx).astype(BF16)
        cb_top = lax.dot_general(c_bf[0:hq], b_bf[0:hq], nt, preferred_element_type=F32)
        cb_bot = lax.dot_general(c_bf[hq:q], b_bf, nt, preferred_element_type=F32)
        shift = LANES - DT_LANE_STRIDE * g
        cs_cols = cs if g == 0 else pltpu.roll(cs, shift, axis=1)
        cs_rows = (cs_src if g == 0 else pltpu.roll(cs_src, shift, axis=1)).T
        pieces = []
        for p in range(HEADS_PER_GROUP // 2):
            xp_ = x_bf[:, p * LANES:(p + 1) * LANES]
            acc = None
            for k in range(2):
                r = 2 * p + k
                d_top = cs_cols[0:hq, r:r + 1] - cs_rows[r:r + 1, 0:hq]
                d_bot = cs_cols[hq:q, r:r + 1] - cs_rows[r:r + 1, :]
                m_top = (jnp.exp(jnp.where(causal_top, d_top, -jnp.inf)) * cb_top).astype(BF16)
                m_bot = (jnp.exp(jnp.where(causal_bot, d_bot, -jnp.inf)) * cb_bot).astype(BF16)
                keep = first_half if k == 0 else jnp.logical_not(first_half)
                xk = jnp.where(keep, xp_, jnp.zeros_like(xp_))
                t = jnp.concatenate([_dot(m_top, xk[0:hq]), _dot(m_bot, xk)], axis=0)
                acc = t if acc is None else acc + t
            pieces.append(acc)
        y = jnp.concatenate(pieces, axis=1)
        s_old = st_scr[g]
        y = y + _dot(c_bf, s_old.astype(BF16)) * ecsx
        st_scr[g] = sdec * s_old + lax.dot_general(b_bf, xd_bf, tn, preferred_element_type=F32)
        y = y + dexp_ref[g] * xs_g
        y = y * sz_ref[:, ch].astype(F32)
        ms = jnp.mean(y * y, axis=-1, keepdims=True)
        return ((y * lax.rsqrt(ms + EPS)) * nw_ref[g]).astype(BF16)

    for p in range(N_GROUPS // 2):
        yn = jnp.concatenate([one_group(2 * p), one_group(2 * p + 1)], axis=1)
        part = _dot(yn, wproj_ref[p])
        if p == 0:
            acc_scr[...] = part
        else:
            acc_scr[...] += part
    o_ref[...] = (acc_scr[...] * sga_ref[...].astype(F32)).astype(o_ref.dtype)


def _mamba(proj, dt_raw, p, batch, seqlen):
    q = CHUNK
    nc = seqlen // q
    t = batch * seqlen

    def rows(b, c):
        return b * nc + c

    def colblk(width, off):
        return pl.BlockSpec((q, width), lambda b, c: (rows(b, c), off // width))

    def full(shape):
        return pl.BlockSpec(shape, lambda b, c: (0,) * len(shape))

    return pl.pallas_call(
        _mamba_body,
        grid=(batch, nc),
        in_specs=[
            colblk(D_INNER, COL_Z), colblk(D_INNER, COL_XS),
            colblk(N_GROUPS * D_STATE, COL_B), colblk(N_GROUPS * D_STATE, COL_C),
            colblk(D_MODEL, COL_GA),
            pl.BlockSpec((q, LANES), lambda b, c: (rows(b, c), 0)),
            full((1, LANES)), full((1, LANES)),
            full((N_GROUPS, 1, GROUP_CH)), full((N_GROUPS, LANES, GROUP_CH)),
            full((N_GROUPS, 1, GROUP_CH)), full((N_GROUPS // 2, 2 * GROUP_CH, D_MODEL)),
        ],
        out_specs=pl.BlockSpec((q, D_MODEL), lambda b, c: (rows(b, c), 0)),
        out_shape=jax.ShapeDtypeStruct((t, D_MODEL), BF16),
        scratch_shapes=[
            pltpu.VMEM((N_GROUPS, D_STATE, GROUP_CH), F32),
            pltpu.VMEM((q, D_MODEL), F32),
        ],
        compiler_params=_cparams(2),
        name="mamba",
    )(proj, proj, proj, proj, proj, dt_raw,
      p["dtb"], p["aneg"], p["dexp"], p["e"], p["nw"], p["wproj"])


S5_BLOCKS = D_S5 // LANES
S5_SLABS = 2 * (S5_NG // S5_BLOCKS) * S5_STATE // LANES
S5_HALF = S5_SLABS // 2
S5_PITCH = 12
S5_UNROLL = 8


def _s5_body(u_ref, gb_ref, ya_ref, wb_ref, wc_ref, lr_ref, li_ref, d_ref, wglu_ref,
             o_ref, xs_scr, st_scr, y_scr, wglu_bf, *, tc):
    @pl.when((pl.program_id(0) == 0) & (pl.program_id(1) == 0))
    def _():
        wglu_bf[...] = wglu_ref[...].astype(BF16)

    @pl.when(pl.program_id(1) == 0)
    def _():
        st_scr[...] = jnp.zeros_like(st_scr)

    u = u_ref[...]
    for s in range(S5_BLOCKS):
        bu = _dot(u[:, s * LANES:(s + 1) * LANES], wb_ref[s])
        for k in range(S5_SLABS):
            xs_scr[k, pl.ds(s, tc, stride=S5_PITCH), :] = bu[:, k * LANES:(k + 1) * LANES]

    lr = [lr_ref[k] for k in range(S5_HALF)]
    li = [li_ref[k] for k in range(S5_HALF)]

    def steps(it, st):
        base = pl.multiple_of(it * (S5_UNROLL * S5_PITCH), SUBLANES)
        for n in range(S5_UNROLL):
            r0 = base + n * S5_PITCH
            new_re, new_im = [], []
            for k in range(S5_HALF):
                re, im = st[k], st[S5_HALF + k]
                nre = (lr[k] * re - li[k] * im) + xs_scr[k, pl.ds(r0, SUBLANES), :]
                nim = (lr[k] * im + li[k] * re) + xs_scr[S5_HALF + k, pl.ds(r0, SUBLANES), :]
                xs_scr[k, pl.ds(r0, SUBLANES), :] = nre
                xs_scr[S5_HALF + k, pl.ds(r0, SUBLANES), :] = nim
                new_re.append(nre)
                new_im.append(nim)
            st = tuple(new_re + new_im)
        return st

    st0 = tuple(st_scr[k] for k in range(S5_SLABS))
    st = lax.fori_loop(0, tc // S5_UNROLL, steps, st0)
    for k in range(S5_SLABS):
        st_scr[k] = st[k]

    for s in range(S5_BLOCKS):
        xb = jnp.concatenate(
            [xs_scr[k, pl.ds(s, tc, stride=S5_PITCH), :] for k in range(S5_SLABS)], axis=1)
        y_scr[:, s * LANES:(s + 1) * LANES] = _dot(xb.astype(BF16), wc_ref[s])

    y = y_scr[...] + d_ref[...] * u.astype(F32)
    y = _gelu_tanh(y).astype(BF16)
    vg = _dot(y, wglu_bf[...])
    yb = vg[:, :D_MODEL] * _sigmoid(vg[:, D_MODEL:])
    merged = ya_ref[...].astype(F32) + gb_ref[...].astype(F32) * yb
    o_ref[...] = merged.astype(o_ref.dtype)


def _s5(proj, ya_gated, p, batch, seqlen, tc=256):
    nc = seqlen // tc
    t = batch * seqlen

    def rows(b, c):
        return b * nc + c

    def full(shape):
        return pl.BlockSpec(shape, lambda b, c: (0,) * len(shape))

    return pl.pallas_call(
        functools.partial(_s5_body, tc=tc),
        grid=(batch, nc),
        in_specs=[
            pl.BlockSpec((tc, D_S5), lambda b, c: (rows(b, c), COL_U // D_S5)),
            pl.BlockSpec((tc, D_MODEL), lambda b, c: (rows(b, c), COL_GB // D_MODEL)),
            pl.BlockSpec((tc, D_MODEL), lambda b, c: (rows(b, c), 0)),
            full((S5_BLOCKS, LANES, S5_SLABS * LANES)),
            full((S5_BLOCKS, S5_SLABS * LANES, LANES)),
            full((S5_HALF, SUBLANES, LANES)), full((S5_HALF, SUBLANES, LANES)),
            full((1, D_S5)),
            pl.BlockSpec((D_S5, 2 * D_MODEL), lambda b, c: (0, 0), pipeline_mode=pl.Buffered(1)),
        ],
        out_specs=pl.BlockSpec((tc, D_MODEL), lambda b, c: (rows(b, c), 0)),
        out_shape=jax.ShapeDtypeStruct((t, D_MODEL), BF16),
        scratch_shapes=[
            pltpu.VMEM((S5_SLABS, tc * S5_PITCH, LANES), F32),
            pltpu.VMEM((S5_SLABS, SUBLANES, LANES), F32),
            pltpu.VMEM((tc, D_S5), F32),
            pltpu.VMEM((D_S5, 2 * D_MODEL), BF16),
        ],
        compiler_params=_cparams(2),
        name="s5",
    )(proj, proj, ya_gated, p["wb"], p["wc"], p["lr"], p["li"], p["d"], p["wglu"])


def _merge_body(x_ref, m_ref, w_ref, o_ref, wbf):
    @pl.when(pl.program_id(0) == 0)
    def _():
        wbf[...] = w_ref[...].astype(BF16)

    o_ref[...] = x_ref[...] + _dot(m_ref[...], wbf[...])


def _merge_proj(x2, merged, w_out, tm=512):
    t = x2.shape[0]
    return pl.pallas_call(
        _merge_body,
        grid=(t // tm,),
        in_specs=[
            pl.BlockSpec((tm, D_MODEL), lambda i: (i, 0)),
            pl.BlockSpec((tm, D_MODEL), lambda i: (i, 0)),
            pl.BlockSpec((D_MODEL, D_MODEL), lambda i: (0, 0), pipeline_mode=pl.Buffered(1)),
        ],
        out_specs=pl.BlockSpec((tm, D_MODEL), lambda i: (i, 0)),
        out_shape=jax.ShapeDtypeStruct((t, D_MODEL), F32),
        scratch_shapes=[pltpu.VMEM((D_MODEL, D_MODEL), BF16)],
        compiler_params=_cparams(1),
        name="merge_proj",
    )(x2, merged, w_out)


def _ffn_body(h_ref, halo_ref, nw_ref, wg_ref, wv_ref, cwg_ref, cwv_ref, cbg_ref, cbv_ref,
              wd_ref, fw_ref, o_ref, lhs_scr, upg_scr, upv_scr, acc_scr, *, tm, tiles_per_seq):
    i = pl.program_id(0)
    j = pl.program_id(1)
    halo = BF16_ROWS

    def norm(v):
        ms = jnp.mean(v * v, axis=-1, keepdims=True)
        return ((v * lax.rsqrt(ms + EPS)) * nw_ref[...]).astype(BF16)

    @pl.when(j == 0)
    def _():
        lhs_scr[halo:halo + tm, :] = norm(h_ref[...])
        first = (i % tiles_per_seq) == 0
        hn_halo = norm(halo_ref[...])
        lhs_scr[0:halo, :] = jnp.where(first, jnp.zeros_like(hn_halo), hn_halo)
        acc_scr[...] = jnp.zeros_like(acc_scr)

    lhs = lhs_scr[...]
    upg_scr[...] = _dot(lhs, wg_ref[...])
    upv_scr[...] = _dot(lhs, wv_ref[...])

    def conv(up_scr, cw_ref, cb_ref):
        acc = cb_ref[...] + cw_ref[2:3, :] * up_scr[pl.ds(halo, tm), :]
        acc = acc + cw_ref[1:2, :] * up_scr[pl.ds(halo - 1, tm), :]
        acc = acc + cw_ref[0:1, :] * up_scr[pl.ds(halo - 2, tm), :]
        return acc

    act = (_silu(conv(upg_scr, cwg_ref, cbg_ref)) * conv(upv_scr, cwv_ref, cbv_ref)).astype(BF16)
    acc_scr[...] += _dot(act, wd_ref[...])

    @pl.when(j == pl.num_programs(1) - 1)
    def _():
        h2 = h_ref[...] + acc_scr[...]
        ms = jnp.mean(h2 * h2, axis=-1, keepdims=True)
        o_ref[...] = (h2 * lax.rsqrt(ms + EPS)) * fw_ref[...]


def _ffn(h1, norm_w, w_up, conv_w, conv_b, w_down, final_w, seqlen, tm=512, tn=512):
    t = h1.shape[0]
    nj = D_FF // tn
    halo = BF16_ROWS
    halo_blocks = tm // halo
    return pl.pallas_call(
        functools.partial(_ffn_body, tm=tm, tiles_per_seq=seqlen // tm),
        grid=(t // tm, nj),
        in_specs=[
            pl.BlockSpec((tm, D_MODEL), lambda i, j: (i, 0)),
            pl.BlockSpec((halo, D_MODEL), lambda i, j: (jnp.maximum(i * halo_blocks - 1, 0), 0)),
            pl.BlockSpec((1, D_MODEL), lambda i, j: (0, 0)),
            pl.BlockSpec((D_MODEL, tn), lambda i, j: (0, j)),
            pl.BlockSpec((D_MODEL, tn), lambda i, j: (0, nj + j)),
            pl.BlockSpec((CONV_FFN, tn), lambda i, j: (0, j)),
            pl.BlockSpec((CONV_FFN, tn), lambda i, j: (0, nj + j)),
            pl.BlockSpec((1, tn), lambda i, j: (0, j)),
            pl.BlockSpec((1, tn), lambda i, j: (0, nj + j)),
            pl.BlockSpec((tn, D_MODEL), lambda i, j: (j, 0)),
            pl.BlockSpec((1, D_MODEL), lambda i, j: (0, 0)),
        ],
        out_specs=pl.BlockSpec((tm, D_MODEL), lambda i, j: (i, 0)),
        out_shape=jax.ShapeDtypeStruct((t, D_MODEL), F32),
        scratch_shapes=[
            pltpu.VMEM((tm + halo, D_MODEL), BF16),
            pltpu.VMEM((tm + halo, tn), F32),
            pltpu.VMEM((tm + halo, tn), F32),
            pltpu.VMEM((tm, D_MODEL), F32),
        ],
        compiler_params=_cparams(2),
        name="ffn",
    )(h1, h1, norm_w, w_up, w_up, conv_w, conv_w, conv_b, conv_b, w_down, final_w)


def _mamba_params(dt_bias, a_log, d_a, norm_w, w_proj):
    def grouped(v, width):
        lead = v.shape[:-1]
        v = v.reshape(lead + (N_GROUPS, width))
        return jnp.moveaxis(v, -2, 0)

    def head_lanes(v):
        v = v.reshape(N_GROUPS, HEADS_PER_GROUP)
        v = jnp.pad(v, ((0, 0), (0, DT_LANE_STRIDE - HEADS_PER_GROUP)))
        return v.reshape(1, LANES)

    lane_head = jnp.arange(LANES)
    ch_head = jnp.arange(GROUP_CH) // HEAD_DIM
    e = []
    for g in range(N_GROUPS):
        e.append((lane_head[:, None] == (DT_LANE_STRIDE * g + ch_head)[None, :]))
    e = jnp.stack(e).astype(BF16)
    return {
        "dtb": head_lanes(dt_bias),
        "aneg": head_lanes(-jnp.exp(a_log)),
        "dexp": grouped(jnp.repeat(d_a, HEAD_DIM)[None, :], GROUP_CH),
        "e": e,
        "nw": grouped(norm_w[None, :], GROUP_CH),
        "wproj": w_proj.reshape(N_GROUPS // 2, 2 * GROUP_CH, D_MODEL),
    }


def _s5_params(lam_re, lam_im, log_dt, b_re, b_im, c_re, c_im, d_skip, w_glu):
    lre = jnp.minimum(lam_re, EIG_MAX)
    lim = lam_im
    dt = jnp.exp(log_dt)[:, None]
    mag = jnp.exp(lre * dt)
    lbr = mag * jnp.cos(lim * dt)
    lbi = mag * jnp.sin(lim * dt)
    nr, ni = lbr - 1.0, lbi
    den = lre * lre + lim * lim
    cr = (nr * lre + ni * lim) / den
    ci = (ni * lre - nr * lim) / den
    bbr = cr[..., None] * b_re - ci[..., None] * b_im
    bbi = cr[..., None] * b_im + ci[..., None] * b_re
    gl = S5_NG // S5_BLOCKS
    eye = jnp.eye(gl, dtype=F32)

    def in_map(v):
        v = v.reshape(S5_BLOCKS, gl, S5_STATE, S5_GROUP).transpose(0, 1, 3, 2)
        v = jnp.einsum("sgcp,gh->sgchp", v, eye)
        return v.reshape(S5_BLOCKS, gl * S5_GROUP, gl * S5_STATE)

    def out_map(v):
        v = v.reshape(S5_BLOCKS, gl, S5_GROUP, S5_STATE).transpose(0, 1, 3, 2)
        v = jnp.einsum("sgpc,gh->sgphc", v, eye)
        return v.reshape(S5_BLOCKS, gl * S5_STATE, gl * S5_GROUP)

    def scan_coef(v):
        return v.reshape(S5_BLOCKS, S5_HALF, LANES).transpose(1, 0, 2)

    return {
        "wb": jnp.concatenate([in_map(bbr), in_map(bbi)], axis=2).astype(BF16),
        "wc": jnp.concatenate([out_map(c_re), -out_map(c_im)], axis=1).astype(BF16),
        "lr": scan_coef(lbr),
        "li": scan_coef(lbi),
        "d": d_skip[None, :],
        "wglu": w_glu,
    }


def _dt_weights(w_t):
    w_dt = w_t[N_DIRECT:N_DIRECT + N_HEADS].T.reshape(D_MODEL, N_GROUPS, HEADS_PER_GROUP)
    w_dt = jnp.pad(w_dt, ((0, 0), (0, 0), (0, DT_LANE_STRIDE - HEADS_PER_GROUP)))
    return w_dt.reshape(D_MODEL, LANES).astype(BF16)


def kernel(x, norm_mix_w, w_in, conv_a_w, conv_a_b, dt_bias, a_log, d_a, norm_a_w, w_proj_a,
           s5_lam_re, s5_lam_im, s5_log_dt, s5_b_re, s5_b_im, s5_c_re, s5_c_im, s5_d, w_s5_glu,
           w_out, norm_ffn_w, w_up, conv_ffn_w, conv_ffn_b, w_down, norm_final_w):
    batch, seqlen, _ = x.shape
    assert w_in.shape[0] == 1, "single-layer block: the final norm is fused into the FFN stage"
    h = x.reshape(batch * seqlen, D_MODEL)
    w_t = jnp.swapaxes(w_in[0], 0, 1)
    hn, dt_raw = _norm_dt(h, norm_mix_w[0][None, :], _dt_weights(w_t))
    proj, (w_proj_bf, w_up_bf, w_down_bf) = _inproj(
        hn, w_t, conv_a_w[0], conv_a_b[0][None, :], (w_proj_a[0],), (w_up[0], w_down[0]), seqlen)
    mp = _mamba_params(dt_bias[0], a_log[0], d_a[0], norm_a_w[0], w_proj_bf)
    ya_gated = _mamba(proj, dt_raw, mp, batch, seqlen)
    sp = _s5_params(s5_lam_re[0], s5_lam_im[0], s5_log_dt[0], s5_b_re[0], s5_b_im[0],
                    s5_c_re[0], s5_c_im[0], s5_d[0], w_s5_glu[0])
    merged = _s5(proj, ya_gated, sp, batch, seqlen)
    h1 = _merge_proj(h, merged, w_out[0])
    out = _ffn(h1, norm_ffn_w[0][None, :], w_up_bf, conv_ffn_w[0],
               conv_ffn_b[0][None, :], w_down_bf, norm_final_w[None, :], seqlen)
    return out.reshape(batch, seqlen, D_MODEL)
```

```python
import functools
import math

import jax
import jax.numpy as jnp
from jax import lax
from jax.experimental import pallas as pl
from jax.experimental.pallas import tpu as pltpu

F32 = jnp.float32
BF16 = jnp.bfloat16

D_MODEL = 2048
D_INNER = 3072
HEAD_DIM = 64
N_HEADS = 48
N_GROUPS = 8
HEADS_PER_GROUP = 6
GROUP_CH = D_INNER // N_GROUPS
D_STATE = 128
CONV_A = 4
CHUNK = 256
D_S5 = 1024
S5_GROUP = 16
S5_NG = 64
S5_STATE = 64
D_FF = 5632
CONV_FFN = 3
EPS = 1e-6
EIG_MAX = -1e-4

LANES = 128
SUBLANES = 8
BF16_ROWS = 16
VMEM_LIMIT = 56 * 1024 * 1024

COL_Z = 0
COL_XS = 3072
COL_B = 6144
COL_C = 7168
COL_GA = 8192
COL_GB = 10240
COL_U = 12288
N_MAIN = 13312
N_XBC = D_INNER + 2 * N_GROUPS * D_STATE
N_DIRECT = D_INNER + N_XBC
DT_LANE_STRIDE = 16
NEG_LOG2E = -1.4426950408889634


def _sigmoid(v):
    return 1.0 / (1.0 + jnp.exp2(v * NEG_LOG2E))


def _silu(v):
    return v * _sigmoid(v)


def _softplus(v):
    return jnp.maximum(v, 0.0) + jnp.log1p(jnp.exp(-jnp.abs(v)))


def _gelu_tanh(v):
    c = math.sqrt(2.0 / math.pi)
    return 0.5 * v * (1.0 + jnp.tanh(c * (v + 0.044715 * (v * v * v))))


def _split3(v):
    hi = v.astype(BF16)
    r1 = v - hi.astype(F32)
    mid = r1.astype(BF16)
    lo = (r1 - mid.astype(F32)).astype(BF16)
    return hi, mid, lo


def _dot(a, b):
    return jnp.dot(a, b, preferred_element_type=F32)


def _dot3(v, m_bf16):
    hi, mid, lo = _split3(v)
    return _dot(hi, m_bf16) + _dot(mid, m_bf16) + _dot(lo, m_bf16)


def _cparams(n_axes):
    return pltpu.CompilerParams(dimension_semantics=("arbitrary",) * n_axes,
                                vmem_limit_bytes=VMEM_LIMIT)


def _norm_body(x_ref, nw_ref, wdt_ref, hn_ref, dt_ref):
    x = x_ref[...]
    ms = jnp.mean(x * x, axis=-1, keepdims=True)
    hn = ((x * lax.rsqrt(ms + EPS)) * nw_ref[...]).astype(BF16)
    hn_ref[...] = hn
    dt_ref[...] = _dot(hn, wdt_ref[...])


def _norm_dt(x2, norm_w, w_dt, tm=1024):
    t = x2.shape[0]
    return pl.pallas_call(
        _norm_body,
        grid=(t // tm,),
        in_specs=[
            pl.BlockSpec((tm, D_MODEL), lambda i: (i, 0)),
            pl.BlockSpec((1, D_MODEL), lambda i: (0, 0)),
            pl.BlockSpec((D_MODEL, LANES), lambda i: (0, 0)),
        ],
        out_specs=[
            pl.BlockSpec((tm, D_MODEL), lambda i: (i, 0)),
            pl.BlockSpec((tm, LANES), lambda i: (i, 0)),
        ],
        out_shape=[
            jax.ShapeDtypeStruct((t, D_MODEL), BF16),
            jax.ShapeDtypeStruct((t, LANES), F32),
        ],
        compiler_params=_cparams(1),
        name="norm_dt",
    )(x2, norm_w, w_dt)


IN_TILE = 1024
IN_SUB = 256
J_XS = COL_XS // IN_TILE
J_GA = COL_GA // IN_TILE
J_U = COL_U // IN_TILE
ROW_U = N_DIRECT + N_HEADS
ROW_GA = ROW_U + D_S5


def _inproj_row(j):
    assert ROW_GA % SUBLANES == 0 and ROW_U % SUBLANES == 0
    tile8 = IN_TILE // SUBLANES
    row8 = jnp.where(j < J_GA, j * tile8,
                     jnp.where(j < J_U, ROW_GA // SUBLANES + (j - J_GA) * tile8, ROW_U // SUBLANES))
    return row8 * SUBLANES


def _inproj_body(hn_ref, wt_ref, cw_ref, cb_ref, *rest, tm, tiles_per_seq, n_ride_z, n_ride_gate):
    n_cast = n_ride_z + n_ride_gate
    cast_in = rest[:n_cast]
    o_ref = rest[n_cast]
    cast_out = rest[n_cast + 1:2 * n_cast + 1]
    wbf, xp, carry = rest[2 * n_cast + 1:]
    j = pl.program_id(0)
    i = pl.program_id(1)
    halo = SUBLANES

    def ride_along(lo, hi):
        for src, dst in zip(cast_in[lo:hi], cast_out[lo:hi]):
            dst[...] = src[...].astype(BF16)

    @pl.when(i == 0)
    def _():
        for n in range(IN_TILE // IN_SUB):
            sl = slice(n * IN_SUB, (n + 1) * IN_SUB)
            wbf[:, sl] = wt_ref[sl, :].T.astype(BF16)

    def tiles(epilogue):
        hn = hn_ref[...]
        for c0 in range(0, IN_TILE, IN_SUB):
            sl = slice(c0, c0 + IN_SUB)
            o_ref[:, sl] = epilogue(_dot(hn, wbf[:, sl]), sl).astype(o_ref.dtype)

    @pl.when(j < J_XS)
    def _():
        ride_along(0, n_ride_z)
        tiles(lambda r, sl: _silu(r))

    @pl.when((j >= J_XS) & (j < J_GA))
    def _():
        first = (i % tiles_per_seq) == 0

        def conv_silu(r, sl):
            before = carry[:, sl]
            xp[0:halo, sl] = jnp.where(first, jnp.zeros_like(before), before)
            xp[halo:halo + tm, sl] = r
            carry[:, sl] = r[tm - halo:tm, :]
            acc = cb_ref[:, sl] + cw_ref[3:4, sl] * r
            acc = acc + cw_ref[2:3, sl] * xp[pl.ds(halo - 1, tm), sl]
            acc = acc + cw_ref[1:2, sl] * xp[pl.ds(halo - 2, tm), sl]
            acc = acc + cw_ref[0:1, sl] * xp[pl.ds(halo - 3, tm), sl]
            return _silu(acc)

        tiles(conv_silu)

    @pl.when((j >= J_GA) & (j < J_U))
    def _():
        ride_along(n_ride_z, n_cast)
        tiles(lambda r, sl: _sigmoid(r))

    @pl.when(j == J_U)
    def _():
        tiles(lambda r, sl: r)


def _inproj(hn, w_t, conv_w, conv_b, ride_z, ride_gate, seqlen, tm=1024):
    t = hn.shape[0]
    n_conv = N_XBC // IN_TILE
    row_tiles = t // tm

    def slab(w, first_step, n_steps):
        rows = w.shape[0] // n_steps
        assert rows * n_steps == w.shape[0] and rows % BF16_ROWS == 0
        return pl.BlockSpec(
            (rows, w.shape[1]),
            lambda j, i: (jnp.clip(j * row_tiles + i - first_step, 0, n_steps - 1), 0))

    later_weights = tuple(ride_z) + tuple(ride_gate)
    slabs = ([slab(w, 0, J_XS * row_tiles) for w in ride_z]
             + [slab(w, J_GA * row_tiles, (J_U - J_GA) * row_tiles) for w in ride_gate])
    outs = pl.pallas_call(
        functools.partial(_inproj_body, tm=tm, tiles_per_seq=seqlen // tm,
                          n_ride_z=len(ride_z), n_ride_gate=len(ride_gate)),
        grid=(N_MAIN // IN_TILE, row_tiles),
        in_specs=[
            pl.BlockSpec((tm, D_MODEL), lambda j, i: (i, 0)),
            pl.BlockSpec((pl.Element(IN_TILE), pl.Element(D_MODEL)),
                         lambda j, i: (_inproj_row(j), 0)),
            pl.BlockSpec((CONV_A, IN_TILE), lambda j, i: (0, jnp.clip(j - J_XS, 0, n_conv - 1))),
            pl.BlockSpec((1, IN_TILE), lambda j, i: (0, jnp.clip(j - J_XS, 0, n_conv - 1))),
        ] + slabs,
        out_specs=[pl.BlockSpec((tm, IN_TILE), lambda j, i: (i, j))] + slabs,
        out_shape=[jax.ShapeDtypeStruct((t, N_MAIN), BF16)]
        + [jax.ShapeDtypeStruct(w.shape, BF16) for w in later_weights],
        scratch_shapes=[
            pltpu.VMEM((D_MODEL, IN_TILE), BF16),
            pltpu.VMEM((tm + SUBLANES, IN_TILE), F32),
            pltpu.VMEM((SUBLANES, IN_TILE), F32),
        ],
        compiler_params=_cparams(2),
        name="inproj",
    )(hn, w_t, conv_w, conv_b, *later_weights)
    return outs[0], outs[1:]


def _mamba_body(sz_ref, xs_ref, b_ref, c_ref, sga_ref, dtraw_ref,
                dtb_ref, aneg_ref, dexp_ref, e_ref, nw_ref, wproj_ref,
                o_ref,
                st_scr, acc_scr):
    q = CHUNK

    @pl.when(pl.program_id(1) == 0)
    def _():
        st_scr[...] = jnp.zeros_like(st_scr)

    dt = _softplus(dtraw_ref[...] + dtb_ref[...])
    a = aneg_ref[...] * dt
    row = lax.broadcasted_iota(jnp.int32, (q, q), 0)
    col = lax.broadcasted_iota(jnp.int32, (q, q), 1)
    causal = row >= col
    tril = jnp.where(causal, 1.0, 0.0).astype(BF16)
    ah, am, al = _split3(a)
    cs = _dot(tril, ah) + _dot(tril, am) + _dot(tril, al)
    ecs = jnp.exp(cs)
    ecs_bf = ecs.astype(BF16)
    wst_bf = (dt * jnp.exp(cs[q - 1:q, :] - cs)).astype(BF16)
    ecs_last = jnp.broadcast_to(ecs[q - 1:q, :], (SUBLANES, LANES))
    cs_src = cs - jnp.log(dt)

    lane = lax.broadcasted_iota(jnp.int32, (q, LANES), 1)
    first_half = lane < HEAD_DIM

    hq = q // 2
    causal_top = causal[0:hq, 0:hq]
    causal_bot = causal[hq:q, :]
    nt = (((1,), (1,)), ((), ()))
    tn = (((0,), (0,)), ((), ()))

    def one_group(g):
        ch = slice(g * GROUP_CH, (g + 1) * GROUP_CH)
        st = slice(g * D_STATE, (g + 1) * D_STATE)
        x_bf = xs_ref[:, ch]
        xs_g = x_bf.astype(F32)
        b_bf = b_ref[:, st]
        c_bf = c_ref[:, st]
        e_g = e_ref[g]
        ecsx = _dot(ecs_bf, e_g)
        wstx = _dot(wst_bf, e_g).astype(BF16)
        sdec = _dot3(ecs_last, e_g)[0:1, :]
        xd_bf = x_bf * wstx
        cb_top = lax.dot_general(c_bf[0:hq], b_bf[0:hq], nt, preferred_element_type=F32)
        cb_bot = lax.dot_general(c_bf[hq:q], b_bf, nt, preferred_element_type=F32)
        shift = LANES - DT_LANE_STRIDE * g
        cs_cols = cs if g == 0 else pltpu.roll(cs, shift, axis=1)
        cs_rows = (cs_src if g == 0 else pltpu.roll(cs_src, shift, axis=1)).T
        pieces = []
        for p in range(HEADS_PER_GROUP // 2):
            xp_ = x_bf[:, p * LANES:(p + 1) * LANES]
            acc = None
            for k in range(2):
                r = 2 * p + k
                d_top = cs_cols[0:hq, r:r + 1] - cs_rows[r:r + 1, 0:hq]
                d_bot = cs_cols[hq:q, r:r + 1] - cs_rows[r:r + 1, :]
                m_top = (jnp.exp(jnp.where(causal_top, d_top, -jnp.inf)) * cb_top).astype(BF16)
                m_bot = (jnp.exp(jnp.where(causal_bot, d_bot, -jnp.inf)) * cb_bot).astype(BF16)
                keep = first_half if k == 0 else jnp.logical_not(first_half)
                xk = jnp.where(keep, xp_, jnp.zeros_like(xp_))
                t = jnp.concatenate([_dot(m_top, xk[0:hq]), _dot(m_bot, xk)], axis=0)
                acc = t if acc is None else acc + t
            pieces.append(acc)
        y = jnp.concatenate(pieces, axis=1)
        s_old = st_scr[g]
        y = y + _dot(c_bf, s_old.astype(BF16)) * ecsx
        st_scr[g] = sdec * s_old + lax.dot_general(b_bf, xd_bf, tn, preferred_element_type=F32)
        y = y + dexp_ref[:, ch] * xs_g
        y = y * sz_ref[:, ch].astype(F32)
        ms = jnp.mean(y * y, axis=-1, keepdims=True)
        return ((y * lax.rsqrt(ms + EPS)) * nw_ref[:, ch]).astype(BF16)

    for p in range(N_GROUPS // 2):
        yn = jnp.concatenate([one_group(2 * p), one_group(2 * p + 1)], axis=1)
        part = _dot(yn, wproj_ref[p])
        if p == 0:
            acc_scr[...] = part
        else:
            acc_scr[...] += part
    o_ref[...] = (acc_scr[...] * sga_ref[...].astype(F32)).astype(o_ref.dtype)


def _mamba(proj, dt_raw, p, batch, seqlen):
    q = CHUNK
    nc = seqlen // q
    t = batch * seqlen

    def rows(b, c):
        return b * nc + c

    def colblk(width, off):
        return pl.BlockSpec((q, width), lambda b, c: (rows(b, c), off // width))

    def full(shape):
        return pl.BlockSpec(shape, lambda b, c: (0,) * len(shape))

    return pl.pallas_call(
        _mamba_body,
        grid=(batch, nc),
        in_specs=[
            colblk(D_INNER, COL_Z), colblk(D_INNER, COL_XS),
            colblk(N_GROUPS * D_STATE, COL_B), colblk(N_GROUPS * D_STATE, COL_C),
            colblk(D_MODEL, COL_GA),
            pl.BlockSpec((q, LANES), lambda b, c: (rows(b, c), 0)),
            full((1, LANES)), full((1, LANES)),
            full((1, D_INNER)), full((N_GROUPS, LANES, GROUP_CH)),
            full((1, D_INNER)), full((N_GROUPS // 2, 2 * GROUP_CH, D_MODEL)),
        ],
        out_specs=pl.BlockSpec((q, D_MODEL), lambda b, c: (rows(b, c), 0)),
        out_shape=jax.ShapeDtypeStruct((t, D_MODEL), BF16),
        scratch_shapes=[
            pltpu.VMEM((N_GROUPS, D_STATE, GROUP_CH), F32),
            pltpu.VMEM((q, D_MODEL), F32),
        ],
        compiler_params=_cparams(2),
        name="mamba",
    )(proj, proj, proj, proj, proj, dt_raw,
      p["dtb"], p["aneg"], p["dexp"], p["e"], p["nw"], p["wproj"])


S5_BLOCKS = D_S5 // LANES
S5_SLABS = 2 * (S5_NG // S5_BLOCKS) * S5_STATE // LANES
S5_HALF = S5_SLABS // 2
S5_PITCH = 12
S5_UNROLL = 8


def _s5_body(u_ref, gb_ref, ya_ref, wb_ref, wc_ref, lr_ref, li_ref, d_ref, wglu_ref,
             o_ref, xs_scr, st_scr, y_scr, wglu_bf, *, tc):
    @pl.when((pl.program_id(0) == 0) & (pl.program_id(1) == 0))
    def _():
        wglu_bf[...] = wglu_ref[...].astype(BF16)

    @pl.when(pl.program_id(1) == 0)
    def _():
        st_scr[...] = jnp.zeros_like(st_scr)

    u = u_ref[...]
    for s in range(S5_BLOCKS):
        bu = _dot(u[:, s * LANES:(s + 1) * LANES], wb_ref[s])
        for k in range(S5_SLABS):
            xs_scr[k, pl.ds(s, tc, stride=S5_PITCH), :] = bu[:, k * LANES:(k + 1) * LANES]

    lr = [lr_ref[k] for k in range(S5_HALF)]
    li = [li_ref[k] for k in range(S5_HALF)]

    def steps(it, st):
        base = pl.multiple_of(it * (S5_UNROLL * S5_PITCH), SUBLANES)
        for n in range(S5_UNROLL):
            r0 = base + n * S5_PITCH
            new_re, new_im = [], []
            for k in range(S5_HALF):
                re, im = st[k], st[S5_HALF + k]
                nre = (lr[k] * re - li[k] * im) + xs_scr[k, pl.ds(r0, SUBLANES), :]
                nim = (lr[k] * im + li[k] * re) + xs_scr[S5_HALF + k, pl.ds(r0, SUBLANES), :]
                xs_scr[k, pl.ds(r0, SUBLANES), :] = nre
                xs_scr[S5_HALF + k, pl.ds(r0, SUBLANES), :] = nim
                new_re.append(nre)
                new_im.append(nim)
            st = tuple(new_re + new_im)
        return st

    st0 = tuple(st_scr[k] for k in range(S5_SLABS))
    st = lax.fori_loop(0, tc // S5_UNROLL, steps, st0)
    for k in range(S5_SLABS):
        st_scr[k] = st[k]

    for s in range(S5_BLOCKS):
        xb = jnp.concatenate(
            [xs_scr[k, pl.ds(s, tc, stride=S5_PITCH), :] for k in range(S5_SLABS)], axis=1)
        y_scr[:, s * LANES:(s + 1) * LANES] = _dot(xb.astype(BF16), wc_ref[s])

    y = y_scr[...] + d_ref[...] * u.astype(F32)
    y = _gelu_tanh(y).astype(BF16)
    vg = _dot(y, wglu_bf[...])
    yb = vg[:, :D_MODEL] * _sigmoid(vg[:, D_MODEL:])
    merged = ya_ref[...].astype(F32) + gb_ref[...].astype(F32) * yb
    o_ref[...] = merged.astype(o_ref.dtype)


def _s5(proj, ya_gated, p, batch, seqlen, tc=256):
    nc = seqlen // tc
    t = batch * seqlen

    def rows(b, c):
        return b * nc + c

    def full(shape):
        return pl.BlockSpec(shape, lambda b, c: (0,) * len(shape))

    return pl.pallas_call(
        functools.partial(_s5_body, tc=tc),
        grid=(batch, nc),
        in_specs=[
            pl.BlockSpec((tc, D_S5), lambda b, c: (rows(b, c), COL_U // D_S5)),
            pl.BlockSpec((tc, D_MODEL), lambda b, c: (rows(b, c), COL_GB // D_MODEL)),
            pl.BlockSpec((tc, D_MODEL), lambda b, c: (rows(b, c), 0)),
            full((S5_BLOCKS, LANES, S5_SLABS * LANES)),
            full((S5_BLOCKS, S5_SLABS * LANES, LANES)),
            full((S5_HALF, SUBLANES, LANES)), full((S5_HALF, SUBLANES, LANES)),
            full((1, D_S5)),
            pl.BlockSpec((D_S5, 2 * D_MODEL), lambda b, c: (0, 0), pipeline_mode=pl.Buffered(1)),
        ],
        out_specs=pl.BlockSpec((tc, D_MODEL), lambda b, c: (rows(b, c), 0)),
        out_shape=jax.ShapeDtypeStruct((t, D_MODEL), BF16),
        scratch_shapes=[
            pltpu.VMEM((S5_SLABS, tc * S5_PITCH, LANES), F32),
            pltpu.VMEM((S5_SLABS, SUBLANES, LANES), F32),
            pltpu.VMEM((tc, D_S5), F32),
            pltpu.VMEM((D_S5, 2 * D_MODEL), BF16),
        ],
        compiler_params=_cparams(2),
        name="s5",
    )(proj, proj, ya_gated, p["wb"], p["wc"], p["lr"], p["li"], p["d"], p["wglu"])


def _merge_body(x_ref, m_ref, w_ref, o_ref, wbf):
    @pl.when(pl.program_id(0) == 0)
    def _():
        wbf[...] = w_ref[...].astype(BF16)

    o_ref[...] = x_ref[...] + _dot(m_ref[...], wbf[...])


def _merge_proj(x2, merged, w_out, tm=512):
    t = x2.shape[0]
    return pl.pallas_call(
        _merge_body,
        grid=(t // tm,),
        in_specs=[
            pl.BlockSpec((tm, D_MODEL), lambda i: (i, 0)),
            pl.BlockSpec((tm, D_MODEL), lambda i: (i, 0)),
            pl.BlockSpec((D_MODEL, D_MODEL), lambda i: (0, 0), pipeline_mode=pl.Buffered(1)),
        ],
        out_specs=pl.BlockSpec((tm, D_MODEL), lambda i: (i, 0)),
        out_shape=jax.ShapeDtypeStruct((t, D_MODEL), F32),
        scratch_shapes=[pltpu.VMEM((D_MODEL, D_MODEL), BF16)],
        compiler_params=_cparams(1),
        name="merge_proj",
    )(x2, merged, w_out)


def _ffn_body(h_ref, halo_ref, nw_ref, wg_ref, wv_ref, cwg_ref, cwv_ref, cbg_ref, cbv_ref,
              wd_ref, fw_ref, o_ref, lhs_scr, upg_scr, upv_scr, acc_scr, *, tm, tiles_per_seq):
    i = pl.program_id(0)
    j = pl.program_id(1)
    halo = BF16_ROWS

    def norm(v):
        ms = jnp.mean(v * v, axis=-1, keepdims=True)
        return ((v * lax.rsqrt(ms + EPS)) * nw_ref[...]).astype(BF16)

    @pl.when(j == 0)
    def _():
        lhs_scr[halo:halo + tm, :] = norm(h_ref[...])
        first = (i % tiles_per_seq) == 0
        hn_halo = norm(halo_ref[...])
        lhs_scr[0:halo, :] = jnp.where(first, jnp.zeros_like(hn_halo), hn_halo)
        acc_scr[...] = jnp.zeros_like(acc_scr)

    lhs = lhs_scr[...]
    upg_scr[...] = _dot(lhs, wg_ref[...])
    upv_scr[...] = _dot(lhs, wv_ref[...])

    def conv(up_scr, cw_ref, cb_ref):
        acc = cb_ref[...] + cw_ref[2:3, :] * up_scr[pl.ds(halo, tm), :]
        acc = acc + cw_ref[1:2, :] * up_scr[pl.ds(halo - 1, tm), :]
        acc = acc + cw_ref[0:1, :] * up_scr[pl.ds(halo - 2, tm), :]
        return acc

    act = (_silu(conv(upg_scr, cwg_ref, cbg_ref)) * conv(upv_scr, cwv_ref, cbv_ref)).astype(BF16)
    acc_scr[...] += _dot(act, wd_ref[...])

    @pl.when(j == pl.num_programs(1) - 1)
    def _():
        h2 = h_ref[...] + acc_scr[...]
        ms = jnp.mean(h2 * h2, axis=-1, keepdims=True)
        o_ref[...] = (h2 * lax.rsqrt(ms + EPS)) * fw_ref[...]


def _ffn(h1, norm_w, w_up, conv_w, conv_b, w_down, final_w, seqlen, tm=512, tn=512):
    t = h1.shape[0]
    nj = D_FF // tn
    halo = BF16_ROWS
    halo_blocks = tm // halo
    return pl.pallas_call(
        functools.partial(_ffn_body, tm=tm, tiles_per_seq=seqlen // tm),
        grid=(t // tm, nj),
        in_specs=[
            pl.BlockSpec((tm, D_MODEL), lambda i, j: (i, 0)),
            pl.BlockSpec((halo, D_MODEL), lambda i, j: (jnp.maximum(i * halo_blocks - 1, 0), 0)),
            pl.BlockSpec((1, D_MODEL), lambda i, j: (0, 0)),
            pl.BlockSpec((D_MODEL, tn), lambda i, j: (0, j)),
            pl.BlockSpec((D_MODEL, tn), lambda i, j: (0, nj + j)),
            pl.BlockSpec((CONV_FFN, tn), lambda i, j: (0, j)),
            pl.BlockSpec((CONV_FFN, tn), lambda i, j: (0, nj + j)),
            pl.BlockSpec((1, tn), lambda i, j: (0, j)),
            pl.BlockSpec((1, tn), lambda i, j: (0, nj + j)),
            pl.BlockSpec((tn, D_MODEL), lambda i, j: (j, 0)),
            pl.BlockSpec((1, D_MODEL), lambda i, j: (0, 0)),
        ],
        out_specs=pl.BlockSpec((tm, D_MODEL), lambda i, j: (i, 0)),
        out_shape=jax.ShapeDtypeStruct((t, D_MODEL), F32),
        scratch_shapes=[
            pltpu.VMEM((tm + halo, D_MODEL), BF16),
            pltpu.VMEM((tm + halo, tn), F32),
            pltpu.VMEM((tm + halo, tn), F32),
            pltpu.VMEM((tm, D_MODEL), F32),
        ],
        compiler_params=_cparams(2),
        name="ffn",
    )(h1, h1, norm_w, w_up, w_up, conv_w, conv_w, conv_b, conv_b, w_down, final_w)


def _mamba_params(dt_bias, a_log, d_a, norm_w, w_proj):
    def head_lanes(v):
        v = v.reshape(N_GROUPS, HEADS_PER_GROUP)
        v = jnp.pad(v, ((0, 0), (0, DT_LANE_STRIDE - HEADS_PER_GROUP)))
        return v.reshape(1, LANES)

    lane_head = jnp.arange(LANES)
    ch_head = jnp.arange(GROUP_CH) // HEAD_DIM
    e = []
    for g in range(N_GROUPS):
        e.append((lane_head[:, None] == (DT_LANE_STRIDE * g + ch_head)[None, :]))
    e = jnp.stack(e).astype(BF16)
    return {
        "dtb": head_lanes(dt_bias),
        "aneg": head_lanes(-jnp.exp(a_log)),
        "dexp": jnp.repeat(d_a, HEAD_DIM)[None, :],
        "e": e,
        "nw": norm_w[None, :],
        "wproj": w_proj.reshape(N_GROUPS // 2, 2 * GROUP_CH, D_MODEL),
    }


def _s5_params(lam_re, lam_im, log_dt, b_re, b_im, c_re, c_im, d_skip, w_glu):
    lre = jnp.minimum(lam_re, EIG_MAX)
    lim = lam_im
    dt = jnp.exp(log_dt)[:, None]
    mag = jnp.exp(lre * dt)
    lbr = mag * jnp.cos(lim * dt)
    lbi = mag * jnp.sin(lim * dt)
    nr, ni = lbr - 1.0, lbi
    den = lre * lre + lim * lim
    cr = (nr * lre + ni * lim) / den
    ci = (ni * lre - nr * lim) / den
    bbr = cr[..., None] * b_re - ci[..., None] * b_im
    bbi = cr[..., None] * b_im + ci[..., None] * b_re
    gl = S5_NG // S5_BLOCKS
    eye = jnp.eye(gl, dtype=F32)

    def in_map(v):
        v = v.reshape(S5_BLOCKS, gl, S5_STATE, S5_GROUP).transpose(0, 1, 3, 2)
        v = jnp.einsum("sgcp,gh->sgchp", v, eye)
        return v.reshape(S5_BLOCKS, gl * S5_GROUP, gl * S5_STATE)

    def out_map(v):
        v = v.reshape(S5_BLOCKS, gl, S5_GROUP, S5_STATE).transpose(0, 1, 3, 2)
        v = jnp.einsum("sgpc,gh->sgphc", v, eye)
        return v.reshape(S5_BLOCKS, gl * S5_STATE, gl * S5_GROUP)

    def scan_coef(v):
        return v.reshape(S5_BLOCKS, S5_HALF, LANES).transpose(1, 0, 2)

    return {
        "wb": jnp.concatenate([in_map(bbr), in_map(bbi)], axis=2).astype(BF16),
        "wc": jnp.concatenate([out_map(c_re), -out_map(c_im)], axis=1).astype(BF16),
        "lr": scan_coef(lbr),
        "li": scan_coef(lbi),
        "d": d_skip[None, :],
        "wglu": w_glu,
    }


def _dt_weights(w_t):
    w_dt = w_t[N_DIRECT:N_DIRECT + N_HEADS].T.reshape(D_MODEL, N_GROUPS, HEADS_PER_GROUP)
    w_dt = jnp.pad(w_dt, ((0, 0), (0, 0), (0, DT_LANE_STRIDE - HEADS_PER_GROUP)))
    return w_dt.reshape(D_MODEL, LANES).astype(BF16)


def kernel(x, norm_mix_w, w_in, conv_a_w, conv_a_b, dt_bias, a_log, d_a, norm_a_w, w_proj_a,
           s5_lam_re, s5_lam_im, s5_log_dt, s5_b_re, s5_b_im, s5_c_re, s5_c_im, s5_d, w_s5_glu,
           w_out, norm_ffn_w, w_up, conv_ffn_w, conv_ffn_b, w_down, norm_final_w):
    batch, seqlen, _ = x.shape
    assert w_in.shape[0] == 1, "single-layer block: the final norm is fused into the FFN stage"
    h = x.reshape(batch * seqlen, D_MODEL)
    w_t = jnp.swapaxes(w_in[0], 0, 1)
    hn, dt_raw = _norm_dt(h, norm_mix_w[0][None, :], _dt_weights(w_t))
    proj, (w_proj_bf, w_up_bf, w_down_bf) = _inproj(
        hn, w_t, conv_a_w[0], conv_a_b[0][None, :], (w_proj_a[0],), (w_up[0], w_down[0]), seqlen)
    mp = _mamba_params(dt_bias[0], a_log[0], d_a[0], norm_a_w[0], w_proj_bf)
    ya_gated = _mamba(proj, dt_raw, mp, batch, seqlen)
    sp = _s5_params(s5_lam_re[0], s5_lam_im[0], s5_log_dt[0], s5_b_re[0], s5_b_im[0],
                    s5_c_re[0], s5_c_im[0], s5_d[0], w_s5_glu[0])
    merged = _s5(proj, ya_gated, sp, batch, seqlen)
    h1 = _merge_proj(h, merged, w_out[0])
    out = _ffn(h1, norm_ffn_w[0][None, :], w_up_bf, conv_ffn_w[0],
               conv_ffn_b[0][None, :], w_down_bf, norm_final_w[None, :], seqlen)
    return out.reshape(batch, seqlen, D_MODEL)
```
